```python
import jax, jax.numpy as jnp
from jax import lax
import numpy as np

D_MODEL = 1024
BATCH = 4
SEQ = 8192
DEPTH = 4
DEC_BATCH = 16
DEC_SEQ = 32
PAST_LEN = 4096

CHUNK = 64
N_MIXERS = 2
N_ATT = (DEPTH + N_MIXERS - 1) // N_MIXERS
N_REC = DEPTH // N_MIXERS
EPS = 1e-6

N_HEADS = 8
ATT_WIDTH = D_MODEL
HEAD_DIM = ATT_WIDTH // N_HEADS
N_KV_HEADS = 2
GROUP = N_HEADS // N_KV_HEADS
KV_WIDTH = N_KV_HEADS * HEAD_DIM
IDX_HEADS = 8
IDX_DIM = 128
MAX_SELECT = 256
QBLK = 128
ROPE_THETA = 10000.0
ATT_SCALE = HEAD_DIM ** -0.5
IDX_SCALE = (IDX_HEADS * IDX_DIM) ** -0.5
ATT_IN = 2 * ATT_WIDTH + 2 * KV_WIDTH + IDX_HEADS * IDX_DIM + IDX_DIM + IDX_HEADS

REC_HEADS = 8
REC_DK = 128
REC_DV = D_MODEL // REC_HEADS
REC_KW = REC_HEADS * REC_DK
REC_WIDTH = REC_HEADS * REC_DV
REC_IN = 2 * REC_KW + 2 * REC_WIDTH

kernel_name = 'dsa_hgrn2_streaming_step'


def split_cols(p, sizes):
    out = []
    off = 0
    for s in sizes:
        out.append(p[..., off:off + s])
        off += s
    return out


def rmsnorm(x, w):
    xf = x.astype(jnp.float32)
    y = xf * lax.rsqrt(jnp.mean(xf * xf, axis=-1, keepdims=True) + EPS)
    return (y * w.astype(jnp.float32)).astype(x.dtype)


def rope(x, pos):
    half = x.shape[-1] // 2
    inv = ROPE_THETA ** (-jnp.arange(half, dtype=jnp.float32) / half)
    ang = pos.astype(jnp.float32)[:, None] * inv[None, :]
    cos = jnp.cos(ang)[None, :, None, :]
    sin = jnp.sin(ang)[None, :, None, :]
    xf = x.astype(jnp.float32)
    x1, x2 = xf[..., :half], xf[..., half:]
    return jnp.concatenate([x1 * cos - x2 * sin, x2 * cos + x1 * sin], axis=-1).astype(x.dtype)


def attn_project(hn, w_in, pos):
    B, L, _ = hn.shape
    p = hn @ w_in
    q, k, v, z, qi, ki, wi = split_cols(p, [ATT_WIDTH, KV_WIDTH, KV_WIDTH, ATT_WIDTH, IDX_HEADS * IDX_DIM, IDX_DIM, IDX_HEADS])
    q = rope(q.reshape(B, L, N_HEADS, HEAD_DIM), pos)
    k = rope(k.reshape(B, L, N_KV_HEADS, HEAD_DIM), pos)
    v = v.reshape(B, L, N_KV_HEADS, HEAD_DIM)
    qi = rope(qi.reshape(B, L, IDX_HEADS, IDX_DIM), pos)
    ki = rope(ki.reshape(B, L, 1, IDX_DIM), pos)[:, :, 0]
    return q, k, v, z, qi, ki, wi


def dsa_attend(q, qi, wi, q_pos, k, v, ki, k_pos, n_sel):
    B, T = q.shape[:2]
    f32 = jnp.float32
    idx_logits = jnp.einsum('bthd,bsd->bths', qi.astype(f32), ki.astype(f32))
    score = jnp.einsum('bth,bths->bts', wi.astype(f32) * IDX_SCALE, jax.nn.relu(idx_logits))
    q_chunk = q_pos // CHUNK
    admissible = (k_pos[None, :] // CHUNK) <= q_chunk[:, None]
    score = jnp.where(admissible[None], score, -jnp.inf)
    _, sel = lax.top_k(score, n_sel)
    valid = (k_pos[sel] // CHUNK) <= q_chunk[None, :, None]
    take = jax.vmap(lambda rows, ids: rows[ids])
    k_sel = take(k, sel)
    v_sel = take(v, sel)
    qg = q.reshape(B, T, N_KV_HEADS, GROUP, HEAD_DIM)
    logits = jnp.einsum('btkgd,btskd->btkgs', qg, k_sel).astype(f32) * ATT_SCALE
    logits = jnp.where(valid[:, :, None, None, :], logits, -jnp.inf)
    probs = jax.nn.softmax(logits, axis=-1).astype(v.dtype)
    o = jnp.einsum('btkgs,btskd->btkgd', probs, v_sel)
    return o.reshape(B, T, N_HEADS, HEAD_DIM)


def attn_out(o, z, w_out):
    B, L = o.shape[:2]
    return (o.reshape(B, L, ATT_WIDTH) * jax.nn.silu(z)) @ w_out


def attn_prompt(hn, w_in, w_out):
    B, L, _ = hn.shape
    pos = jnp.arange(L, dtype=jnp.int32)
    q, k, v, z, qi, ki, wi = attn_project(hn, w_in, pos)
    n_sel = min(MAX_SELECT, L // 4)

    def block(b):
        start = b * QBLK
        sl = lambda a: lax.dynamic_slice_in_dim(a, start, QBLK, axis=1)
        qpos = start + jnp.arange(QBLK, dtype=jnp.int32)
        return dsa_attend(sl(q), sl(qi), sl(wi), qpos, k, v, ki, pos, n_sel)

    o = lax.map(block, jnp.arange(L // QBLK, dtype=jnp.int32))
    o = jnp.moveaxis(o, 0, 1).reshape(B, L, N_HEADS, HEAD_DIM)
    return attn_out(o, z, w_out), k, v, ki


def attn_sample(hn, w_in, w_out, ck, cv, cki):
    B, T, _ = hn.shape
    past = ck.shape[1]
    pos = past + jnp.arange(T, dtype=jnp.int32)
    q, k, v, z, qi, ki, wi = attn_project(hn, w_in, pos)
    k_all = jnp.concatenate([ck.astype(k.dtype), k], axis=1)
    v_all = jnp.concatenate([cv.astype(v.dtype), v], axis=1)
    ki_all = jnp.concatenate([cki.astype(ki.dtype), ki], axis=1)
    k_pos = jnp.arange(past + T, dtype=jnp.int32)
    n_sel = min(MAX_SELECT, (past + T) // 4)
    o = dsa_attend(q, qi, wi, pos, k_all, v_all, ki_all, k_pos, n_sel)
    return attn_out(o, z, w_out), k, v, ki


def rec_project(hn, w_in, lb):
    B, L, _ = hn.shape
    p = hn @ w_in
    q, fp, vi, z = split_cols(p, [REC_KW, REC_KW, REC_WIDTH, REC_WIDTH])
    f = lb + (1.0 - lb) * jax.nn.sigmoid(fp.astype(jnp.float32))
    q = jax.nn.silu(q.astype(jnp.float32)).reshape(B, L, REC_HEADS, REC_DK)
    g = jnp.log(f).reshape(B, L, REC_HEADS, REC_DK)
    k = (1.0 - f).reshape(B, L, REC_HEADS, REC_DK)
    v = vi.astype(jnp.float32).reshape(B, L, REC_HEADS, REC_DV)
    return q, k, v, g, z


def rec_chunk(S0, q, k, v, g):
    C = q.shape[1]
    b = jnp.cumsum(g, axis=1)
    causal = jnp.tril(jnp.ones((C, C), dtype=bool))
    diff = jnp.minimum(b[:, :, None] - b[:, None, :], 0.0)
    decay = jnp.where(causal[None, :, :, None, None], jnp.exp(diff), 0.0)
    scores = jnp.einsum('bthk,bshk,btshk->bhts', q, k, decay)
    o = jnp.einsum('bhts,bshv->bthv', scores, v) + jnp.einsum('bthk,bhkv->bthv', q * jnp.exp(b), S0)
    b_last = b[:, -1]
    S = jnp.exp(b_last)[..., None] * S0 + jnp.einsum('bshk,bshv->bhkv', k * jnp.exp(b_last[:, None] - b), v)
    return o, S


def rec_out(o, z, gnorm, w_out, dtype):
    B, L = o.shape[:2]
    o = o * lax.rsqrt(jnp.mean(o * o, axis=-1, keepdims=True) + EPS) * gnorm.astype(jnp.float32)
    o = o.reshape(B, L, REC_WIDTH) * jax.nn.silu(z.astype(jnp.float32))
    return o.astype(dtype) @ w_out


def rec_prompt(hn, w_in, w_out, gnorm, lb):
    B, L, _ = hn.shape
    q, k, v, g, z = rec_project(hn, w_in, lb)
    S0 = jnp.zeros((B, REC_HEADS, REC_DK, REC_DV), jnp.float32)
    n = L // CHUNK
    to_chunks = lambda a: jnp.moveaxis(a.reshape((B, n, CHUNK) + a.shape[2:]), 1, 0)

    def step(S, xs):
        o, S = rec_chunk(S, *xs)
        return S, o

    S, o = lax.scan(step, S0, (to_chunks(q), to_chunks(k), to_chunks(v), to_chunks(g)))
    o = jnp.moveaxis(o, 0, 1).reshape(B, L, REC_HEADS, REC_DV)
    return rec_out(o, z, gnorm, w_out, hn.dtype), S.astype(hn.dtype)


def rec_sample(hn, w_in, w_out, gnorm, lb, S0):
    q, k, v, g, z = rec_project(hn, w_in, lb)
    o, S = rec_chunk(S0.astype(jnp.float32), q, k, v, g)
    return rec_out(o, z, gnorm, w_out, hn.dtype), S.astype(S0.dtype)


def setup_inputs(seed: int = 0) -> dict:
    key = jax.random.key(seed)
    ks = jax.random.split(key, 14)
    nrm = lambda kk, shape, scale: jax.random.normal(kk, shape, jnp.float32) * scale
    return {
        'x_prompt': nrm(ks[0], (BATCH, SEQ, D_MODEL), 1.0),
        'x_sample': nrm(ks[1], (DEC_BATCH, DEC_SEQ, D_MODEL), 1.0),
        'cache_k': nrm(ks[2], (N_ATT, DEC_BATCH, PAST_LEN, N_KV_HEADS, HEAD_DIM), 1.0),
        'cache_v': nrm(ks[3], (N_ATT, DEC_BATCH, PAST_LEN, N_KV_HEADS, HEAD_DIM), 1.0),
        'cache_kidx': nrm(ks[4], (N_ATT, DEC_BATCH, PAST_LEN, IDX_DIM), 1.0),
        'state_rec': nrm(ks[5], (N_REC, DEC_BATCH, REC_HEADS, REC_DK, REC_DV), 0.3),
        'norm_w': 1.0 + nrm(ks[6], (DEPTH, D_MODEL), 0.02),
        'final_norm_w': 1.0 + nrm(ks[7], (D_MODEL,), 0.02),
        'att_w_in': nrm(ks[8], (N_ATT, D_MODEL, ATT_IN), D_MODEL ** -0.5),
        'att_w_out': nrm(ks[9], (N_ATT, ATT_WIDTH, D_MODEL), ATT_WIDTH ** -0.5),
        'rec_w_in': nrm(ks[10], (N_REC, D_MODEL, REC_IN), D_MODEL ** -0.5),
        'rec_w_out': nrm(ks[11], (N_REC, REC_WIDTH, D_MODEL), REC_WIDTH ** -0.5),
        'rec_gnorm_w': 1.0 + nrm(ks[12], (N_REC, REC_HEADS, REC_DV), 0.02),
        'rec_lb_logits': nrm(ks[13], (N_REC, REC_KW), 0.5),
    }


def reference(x_prompt, x_sample, cache_k, cache_v, cache_kidx, state_rec, norm_w, final_norm_w,
              att_w_in, att_w_out, rec_w_in, rec_w_out, rec_gnorm_w, rec_lb_logits):
    p_lb = jax.nn.softmax(rec_lb_logits.astype(jnp.float32), axis=0)
    lb = jnp.cumsum(p_lb, axis=0) - p_lb[0]

    h = x_prompt
    kp, vp, kip, sp = [], [], [], []
    for i in range(DEPTH):
        j = i // N_MIXERS
        hn = rmsnorm(h, norm_w[i])
        if i % N_MIXERS == 0:
            out, k, v, ki = attn_prompt(hn, att_w_in[j], att_w_out[j])
            kp.append(k)
            vp.append(v)
            kip.append(ki)
        else:
            out, s = rec_prompt(hn, rec_w_in[j], rec_w_out[j], rec_gnorm_w[j], lb[j])
            sp.append(s)
        h = h + out
    y_prompt = rmsnorm(h, final_norm_w)

    h = x_sample
    ks_, vs_, kis_, ss_ = [], [], [], []
    for i in range(DEPTH):
        j = i // N_MIXERS
        hn = rmsnorm(h, norm_w[i])
        if i % N_MIXERS == 0:
            out, k, v, ki = attn_sample(hn, att_w_in[j], att_w_out[j], cache_k[j], cache_v[j], cache_kidx[j])
            ks_.append(k)
            vs_.append(v)
            kis_.append(ki)
        else:
            out, s = rec_sample(hn, rec_w_in[j], rec_w_out[j], rec_gnorm_w[j], lb[j], state_rec[j])
            ss_.append(s)
        h = h + out
    y_sample = rmsnorm(h, final_norm_w)

    k_prompt = jnp.stack(kp)
    v_prompt = jnp.stack(vp)
    kidx_prompt = jnp.stack(kip)
    state_rec_prompt = jnp.stack(sp)
    k_sample = jnp.stack(ks_)
    v_sample = jnp.stack(vs_)
    kidx_sample = jnp.stack(kis_)
    state_rec_sample = jnp.stack(ss_)
    return (y_prompt, y_sample, k_prompt, v_prompt, kidx_prompt, state_rec_prompt,
            k_sample, v_sample, kidx_sample, state_rec_sample)
```

```python
import functools

import jax
import jax.numpy as jnp
from jax import lax
from jax.experimental import pallas as pl
from jax.experimental.pallas import tpu as pltpu

F32 = jnp.float32
BF16 = jnp.bfloat16
I32 = jnp.int32

EPS = 1e-6
CHUNK = 64
LANES = 128

N_HEADS = 8
HEAD_DIM = 128
N_KV_HEADS = 2
GROUP = N_HEADS // N_KV_HEADS
IDX_HEADS = 8
IDX_DIM = 128
MAX_SELECT = 256
ROPE_THETA = 10000.0
ATT_SCALE = HEAD_DIM ** -0.5
IDX_SCALE = (IDX_HEADS * IDX_DIM) ** -0.5
ATT_WIDTH = N_HEADS * HEAD_DIM
KV_WIDTH = N_KV_HEADS * HEAD_DIM

REC_HEADS = 8
REC_DK = 128
REC_DV = 128
SUB = 16

VMEM_LIMIT = 56 * 1024 * 1024
MASKED = -1e30
INT_MIN = -2 ** 31
INT_MAX = 2 ** 31 - 1


def _params(*sem):
    return pltpu.CompilerParams(dimension_semantics=sem, vmem_limit_bytes=VMEM_LIMIT)


def _sigmoid(x):
    return 1.0 / (1.0 + jnp.exp(-x))


def _rms_scale(x, w):
    ms = jnp.mean(x * x, axis=-1, keepdims=True)
    return x * lax.rsqrt(ms + EPS) * w


def _dot_nt(a, b):
    return lax.dot_general(a, b, (((1,), (1,)), ((), ())), preferred_element_type=F32)


def _dot_tn(a, b):
    return lax.dot_general(a, b, (((0,), (0,)), ((), ())), preferred_element_type=F32)


def _att_in_kernel(x_ref, nw_ref, w_ref, cos_ref, sin_ref,
                   q_ref, qi_ref, wi_ref, z_ref, k_ref, v_ref, ki_ref, kb_ref, vb_ref, kib_ref):
    xn = _rms_scale(x_ref[...], nw_ref[...]).astype(BF16)
    cos = cos_ref[...]
    sin = sin_ref[...]

    def rope(t):
        return t * cos + pltpu.roll(t, HEAD_DIM // 2, 1) * sin

    def proj(c0, c1):
        return jnp.dot(xn, w_ref[:, c0:c1], preferred_element_type=F32)

    c = 0
    pq = proj(c, c + ATT_WIDTH)
    for h in range(N_HEADS):
        q_ref[h] = rope(pq[:, h * HEAD_DIM:(h + 1) * HEAD_DIM]).astype(BF16)
    c += ATT_WIDTH
    pkv = proj(c, c + 2 * KV_WIDTH)
    for g in range(N_KV_HEADS):
        kg = rope(pkv[:, g * HEAD_DIM:(g + 1) * HEAD_DIM])
        k_ref[:, g * HEAD_DIM:(g + 1) * HEAD_DIM] = kg
        kb_ref[:, g * HEAD_DIM:(g + 1) * HEAD_DIM] = kg.astype(BF16)
    vv = pkv[:, KV_WIDTH:]
    v_ref[...] = vv
    vb_ref[...] = vv.astype(BF16)
    c += 2 * KV_WIDTH
    z_ref[...] = proj(c, c + ATT_WIDTH)
    c += ATT_WIDTH
    pqi = proj(c, c + IDX_HEADS * IDX_DIM)
    for h in range(IDX_HEADS):
        qi_ref[h] = rope(pqi[:, h * IDX_DIM:(h + 1) * IDX_DIM]).astype(BF16)
    c += IDX_HEADS * IDX_DIM
    pt = proj(c, c + 2 * LANES)
    ki = rope(pt[:, :IDX_DIM])
    ki_ref[...] = ki
    kib_ref[...] = ki.astype(BF16)
    wi_ref[...] = pt[:, IDX_DIM:IDX_DIM + IDX_HEADS] * IDX_SCALE


def _att_in_proj(x, nw, w_pad, cos, sin, tm):
    B, L, D = x.shape
    grid = (B, L // tm)
    tok = lambda width: pl.BlockSpec((None, tm, width), lambda b, i: (b, i, 0))
    hm = pl.BlockSpec((None, N_HEADS, tm, HEAD_DIM), lambda b, i: (b, 0, i, 0))
    out_shape = (
        jax.ShapeDtypeStruct((B, N_HEADS, L, HEAD_DIM), BF16),
        jax.ShapeDtypeStruct((B, IDX_HEADS, L, IDX_DIM), BF16),
        jax.ShapeDtypeStruct((B, L, IDX_HEADS), F32),
        jax.ShapeDtypeStruct((B, L, ATT_WIDTH), F32),
        jax.ShapeDtypeStruct((B, L, KV_WIDTH), F32),
        jax.ShapeDtypeStruct((B, L, KV_WIDTH), F32),
        jax.ShapeDtypeStruct((B, L, IDX_DIM), F32),
        jax.ShapeDtypeStruct((B, L, KV_WIDTH), BF16),
        jax.ShapeDtypeStruct((B, L, KV_WIDTH), BF16),
        jax.ShapeDtypeStruct((B, L, IDX_DIM), BF16),
    )
    out_specs = (hm, hm, tok(IDX_HEADS), tok(ATT_WIDTH), tok(KV_WIDTH), tok(KV_WIDTH), tok(IDX_DIM),
                 tok(KV_WIDTH), tok(KV_WIDTH), tok(IDX_DIM))
    return pl.pallas_call(
        _att_in_kernel,
        out_shape=out_shape,
        grid=grid,
        in_specs=[
            tok(D),
            pl.BlockSpec((1, D), lambda b, i: (0, 0)),
            pl.BlockSpec(w_pad.shape, lambda b, i: (0, 0)),
            pl.BlockSpec((tm, HEAD_DIM), lambda b, i: (i, 0)),
            pl.BlockSpec((tm, HEAD_DIM), lambda b, i: (i, 0)),
        ],
        out_specs=out_specs,
        compiler_params=_params("parallel", "parallel"),
        name="att_in_proj",
    )(x, nw.reshape(1, D), w_pad, cos, sin)


def _dsa_kernel(q_ref, qi_ref, wi_ref, z_ref, ki_ref, k_ref, v_ref, o_ref,
                key_ref, jfin_ref, m_ref, l_ref, acc_ref,
                *, tq, tk, q_pos_base, l_valid, n_sel, idx_bits):
    i = pl.program_id(1)
    q0 = q_pos_base + i * tq
    row_pos = q0 + lax.broadcasted_iota(I32, (tq, 1), 0)
    row_lim = jnp.minimum((row_pos // CHUNK + 1) * CHUNK, l_valid)
    last_lim = jnp.minimum(((q0 + tq - 1) // CHUNK + 1) * CHUNK, l_valid)
    nk = (last_lim + tk - 1) // tk
    lane_pos = lax.broadcasted_iota(I32, (1, tk), 1)

    qi = qi_ref[...].reshape(IDX_HEADS * tq, IDX_DIM)
    wi = wi_ref[...]

    def score_tile(j, carry):
        r = pl.multiple_of(j * tk, tk)
        kt = ki_ref[pl.ds(r, tk), :].astype(BF16)
        s = _dot_nt(qi, kt)
        acc = jnp.zeros((tq, tk), F32)
        for h in range(IDX_HEADS):
            acc = acc + wi[:, h:h + 1] * jnp.maximum(s[h * tq:(h + 1) * tq], 0.0)
        adm = (j * tk + lane_pos) < row_lim
        sc = jnp.where(adm, acc, -jnp.inf)
        sc = jnp.where(sc == 0.0, 0.0, sc)
        bits = lax.bitcast_convert_type(sc, I32)
        key_ref[j] = bits ^ ((bits >> 31) & INT_MAX)
        return carry

    lax.fori_loop(0, nk, score_tile, 0)

    def count_rows(pred):
        def body(j, cnt):
            for c in range(tk // LANES):
                kk = key_ref[j, :, c * LANES:(c + 1) * LANES]
                cnt = cnt + pred(kk, j * tk + c * LANES).astype(I32)
            return cnt
        cnt = lax.fori_loop(0, nk, body, jnp.zeros((tq, LANES), I32))
        return jnp.sum(cnt, axis=1, keepdims=True)

    def search_bit(it, carry):
        t_biased, c_ge = carry
        cand = t_biased | lax.shift_left(jnp.int32(1), 31 - it)
        cand_b = jnp.broadcast_to(cand ^ INT_MIN, (tq, LANES))
        c = count_rows(lambda kk, base: kk >= cand_b)
        ok = c >= n_sel
        return jnp.where(ok, cand, t_biased), jnp.where(ok, c, c_ge)

    t_biased, c_ge = lax.fori_loop(
        0, 32, search_bit, (jnp.zeros((tq, 1), I32), jnp.full((tq, 1), 0, I32) + nk * tk))
    thr = t_biased ^ INT_MIN
    thr_b = jnp.broadcast_to(thr, (tq, LANES))

    jfin_ref[...] = jnp.full((tq, 1), INT_MAX, I32)

    @pl.when(jnp.max(c_ge) > n_sel)
    def _():
        c_gt = count_rows(lambda kk, base: kk > thr_b)
        need = n_sel - c_gt
        lane_idx = lax.broadcasted_iota(I32, (tq, LANES), 1)

        def idx_bit(it, jlo):
            cand = jlo | lax.shift_left(jnp.int32(1), idx_bits - 1 - it)
            cand_b = jnp.broadcast_to(cand, (tq, LANES))
            c = count_rows(lambda kk, base: (kk == thr_b) & ((lane_idx + base) < cand_b))
            return jnp.where(c < need, cand, jlo)

        jlo = lax.fori_loop(0, idx_bits, idx_bit, jnp.zeros((tq, 1), I32))
        jfin_ref[...] = jnp.where(c_ge > n_sel, jlo, INT_MAX)

    jfin = jfin_ref[...]
    m_ref[...] = jnp.full(m_ref.shape, -jnp.inf, F32)
    l_ref[...] = jnp.zeros(l_ref.shape, F32)
    acc_ref[...] = jnp.zeros(acc_ref.shape, F32)

    def attend_tile(j, carry):
        r = pl.multiple_of(j * tk, tk)
        kk = key_ref[j]
        kpos = j * tk + lane_pos
        sel = (kk > thr) | ((kk == thr) & (kpos <= jfin))
        sel = sel & (kpos < row_lim)
        kt = k_ref[pl.ds(r, tk), :].astype(BF16)
        vt = v_ref[pl.ds(r, tk), :].astype(BF16)
        for g in range(N_KV_HEADS):
            qg = q_ref[g * GROUP:(g + 1) * GROUP].reshape(GROUP * tq, HEAD_DIM)
            s = _dot_nt(qg, kt[:, g * HEAD_DIM:(g + 1) * HEAD_DIM]) * ATT_SCALE
            s = jnp.where(sel[None], s.reshape(GROUP, tq, tk), MASKED).reshape(GROUP * tq, tk)
            m_prev = m_ref[g]
            m_new = jnp.maximum(m_prev, jnp.max(s, axis=-1, keepdims=True))
            alpha = jnp.exp(m_prev - m_new)
            p = jnp.exp(s - m_new)
            l_ref[g] = alpha * l_ref[g] + jnp.sum(p, axis=-1, keepdims=True)
            pv = jnp.dot(p.astype(BF16), vt[:, g * HEAD_DIM:(g + 1) * HEAD_DIM], preferred_element_type=F32)
            acc_ref[g] = alpha * acc_ref[g] + pv
            m_ref[g] = m_new
        return carry

    lax.fori_loop(0, nk, attend_tile, 0)

    for g in range(N_KV_HEADS):
        o = acc_ref[g] / l_ref[g]
        for u in range(GROUP):
            h = g * GROUP + u
            z = z_ref[:, h * HEAD_DIM:(h + 1) * HEAD_DIM]
            gated = o[u * tq:(u + 1) * tq] * (z * _sigmoid(z))
            o_ref[:, h * HEAD_DIM:(h + 1) * HEAD_DIM] = gated.astype(BF16)


def _dsa_attention(q_hm, qi_hm, wi, z, ki, k, v, *, n_batch, n_qblk, tq, tk, per_batch_rows,
                   q_pos_base, l_valid):
    lk = ki.shape[1]
    n_sel = min(MAX_SELECT, l_valid // 4)
    idx_bits = max(1, (lk - 1).bit_length())
    if per_batch_rows:
        qmap = lambda b, i: (b, 0, i, 0)
        rmap = lambda b, i: (b, i, 0)
    else:
        qmap = lambda b, i: (0, 0, b, 0)
        rmap = lambda b, i: (0, b, 0)
    kmap = lambda b, i: (b, 0, 0)
    kern = functools.partial(_dsa_kernel, tq=tq, tk=tk, q_pos_base=q_pos_base, l_valid=l_valid,
                             n_sel=n_sel, idx_bits=idx_bits)
    return pl.pallas_call(
        kern,
        out_shape=jax.ShapeDtypeStruct(z.shape, BF16),
        grid=(n_batch, n_qblk),
        in_specs=[
            pl.BlockSpec((None, N_HEADS, tq, HEAD_DIM), qmap),
            pl.BlockSpec((None, IDX_HEADS, tq, IDX_DIM), qmap),
            pl.BlockSpec((None, tq, IDX_HEADS), rmap),
            pl.BlockSpec((None, tq, ATT_WIDTH), rmap),
            pl.BlockSpec((None, lk, IDX_DIM), kmap),
            pl.BlockSpec((None, lk, KV_WIDTH), kmap),
            pl.BlockSpec((None, lk, KV_WIDTH), kmap),
        ],
        out_specs=pl.BlockSpec((None, tq, ATT_WIDTH), rmap),
        scratch_shapes=[
            pltpu.VMEM((lk // tk, tq, tk), I32),
            pltpu.VMEM((tq, 1), I32),
            pltpu.VMEM((N_KV_HEADS, GROUP * tq, 1), F32),
            pltpu.VMEM((N_KV_HEADS, GROUP * tq, 1), F32),
            pltpu.VMEM((N_KV_HEADS, GROUP * tq, HEAD_DIM), F32),
        ],
        compiler_params=_params("parallel", "arbitrary"),
        name="dsa_attention",
    )(q_hm, qi_hm, wi, z, ki, k, v)


def _rec_in_kernel(x_ref, nw_ref, w_ref, p_ref, *, col_chunk):
    xn = _rms_scale(x_ref[...], nw_ref[...]).astype(BF16)
    for c in range(0, w_ref.shape[1], col_chunk):
        p_ref[:, c:c + col_chunk] = jnp.dot(xn, w_ref[:, c:c + col_chunk], preferred_element_type=F32)


def _rec_in_proj(x, nw, w, tm):
    B, L, D = x.shape
    n_out = w.shape[1]
    return pl.pallas_call(
        functools.partial(_rec_in_kernel, col_chunk=1024),
        out_shape=jax.ShapeDtypeStruct((B, L, n_out), F32),
        grid=(B, L // tm),
        in_specs=[
            pl.BlockSpec((None, tm, D), lambda b, i: (b, i, 0)),
            pl.BlockSpec((1, D), lambda b, i: (0, 0)),
            pl.BlockSpec(w.shape, lambda b, i: (0, 0)),
        ],
        out_specs=pl.BlockSpec((None, tm, n_out), lambda b, i: (b, i, 0)),
        compiler_params=_params("parallel", "parallel"),
        name="rec_in_proj",
    )(x, nw.reshape(1, D), w)


def _rec_kernel(qp_ref, fp_ref, v_ref, z_ref, lbl_ref, gn_ref, s0_ref, o_ref, sout_ref,
                st_ref, q_s, k_s, b_s, v_s, *, layer, ch, n_ch):
    t = pl.program_id(2)

    @pl.when(t == 0)
    def _():
        st_ref[...] = s0_ref[...].T

    logits = lbl_ref[...]
    e = jnp.exp(logits - jnp.max(logits, axis=0, keepdims=True))
    p_lb = e / jnp.sum(e, axis=0, keepdims=True)
    lb = jnp.zeros((1, REC_DK), F32)
    for r in range(1, layer + 1):
        lb = lb + p_lb[r:r + 1]
    gn = gn_ref[...]

    tri = (lax.broadcasted_iota(I32, (ch, ch), 0) >= lax.broadcasted_iota(I32, (ch, ch), 1)).astype(F32)
    sub_row = lax.broadcasted_iota(I32, (SUB, 1), 0)
    ch_row = lax.broadcasted_iota(I32, (ch, 1), 0)

    def chunk(c, carry):
        r = pl.multiple_of(c * ch, ch)
        qp = qp_ref[pl.ds(r, ch), :]
        f = lb + (1.0 - lb) * _sigmoid(fp_ref[pl.ds(r, ch), :])
        q = qp * _sigmoid(qp)
        g = jnp.log(f)
        kk = 1.0 - f
        v = v_ref[pl.ds(r, ch), :]
        b = jnp.dot(tri, g, preferred_element_type=F32, precision=lax.Precision.HIGHEST)
        q_s[...] = q
        k_s[...] = kk
        b_s[...] = b
        v_s[...] = v
        st = st_ref[...]
        vb = v.astype(BF16)

        o_inter = _dot_nt((q * jnp.exp(b)).astype(BF16), st.astype(BF16))

        for blk in range(ch // SUB):
            r0 = blk * SUB
            q_i = q[r0:r0 + SUB]
            b_i = b[r0:r0 + SUB]
            o_i = o_inter[r0:r0 + SUB]
            for s in range(SUB):
                d = jnp.minimum(b_i - b_s[r0 + s:r0 + s + 1, :], 0.0)
                w = jnp.sum(q_i * k_s[r0 + s:r0 + s + 1, :] * jnp.exp(d), axis=-1, keepdims=True)
                w = jnp.where(sub_row >= s, w, 0.0)
                o_i = o_i + w * v_s[r0 + s:r0 + s + 1, :]
            if blk > 0:
                c_i = b[r0 - 1:r0]
                a = (q_i * jnp.exp(b_i - c_i)).astype(BF16)
                kd = jnp.where(ch_row < r0, kk * jnp.exp(jnp.minimum(c_i - b, 0.0)), 0.0).astype(BF16)
                pm = _dot_nt(a, kd)
                o_i = o_i + jnp.dot(pm.astype(BF16), vb, preferred_element_type=F32)
            o_n = o_i * lax.rsqrt(jnp.mean(o_i * o_i, axis=-1, keepdims=True) + EPS) * gn
            z = z_ref[pl.ds(r + r0, SUB), :]
            o_ref[pl.ds(r + r0, SUB), :] = (o_n * (z * _sigmoid(z))).astype(o_ref.dtype)

        b_last = b[ch - 1:ch]
        kd = (kk * jnp.exp(b_last - b)).astype(BF16)
        st_ref[...] = st * jnp.exp(b_last) + _dot_tn(vb, kd)
        return carry

    lax.fori_loop(0, n_ch, chunk, 0)

    @pl.when(t == pl.num_programs(2) - 1)
    def _():
        sout_ref[...] = st_ref[...].T


def _rec_scan(p, lb_logits, gnorm, s0, *, layer, tl, ch):
    B, L, _ = p.shape
    H = REC_HEADS
    col = lambda off: pl.BlockSpec((None, tl, REC_DK), lambda b, h, t: (b, t, off + h))
    n_rec = lb_logits.shape[0]
    return pl.pallas_call(
        functools.partial(_rec_kernel, layer=layer, ch=ch, n_ch=tl // ch),
        out_shape=(jax.ShapeDtypeStruct((B, L, H * REC_DV), BF16),
                   jax.ShapeDtypeStruct(s0.shape, F32)),
        grid=(B, H, L // tl),
        in_specs=[
            col(0), col(H), col(2 * H), col(3 * H),
            pl.BlockSpec((n_rec, REC_DK), lambda b, h, t: (0, h)),
            pl.BlockSpec((1, REC_DV), lambda b, h, t: (0, h)),
            pl.BlockSpec((None, None, REC_DK, REC_DV), lambda b, h, t: (b, h, 0, 0)),
        ],
        out_specs=(pl.BlockSpec((None, tl, REC_DV), lambda b, h, t: (b, t, h)),
                   pl.BlockSpec((None, None, REC_DK, REC_DV), lambda b, h, t: (b, h, 0, 0))),
        scratch_shapes=[pltpu.VMEM((REC_DV, REC_DK), F32)] + [pltpu.VMEM((ch, REC_DK), F32)] * 4,
        compiler_params=_params("parallel", "parallel", "arbitrary"),
        name="rec_scan",
    )(p, p, p, p, lb_logits, gnorm.reshape(1, H * REC_DV), s0)


def _out_kernel(a_ref, w_ref, h_ref, fw_ref, o_ref, *, final_norm):
    h = h_ref[...] + jnp.dot(a_ref[...].astype(BF16), w_ref[...], preferred_element_type=F32)
    if final_norm:
        h = _rms_scale(h, fw_ref[...])
    o_ref[...] = h


def _out_proj(a, w, h, fw, tm, final_norm):
    B, L, D = h.shape
    tok = lambda width: pl.BlockSpec((None, tm, width), lambda b, i: (b, i, 0))
    return pl.pallas_call(
        functools.partial(_out_kernel, final_norm=final_norm),
        out_shape=jax.ShapeDtypeStruct(h.shape, F32),
        grid=(B, L // tm),
        in_specs=[tok(a.shape[-1]), pl.BlockSpec(w.shape, lambda b, i: (0, 0)), tok(D),
                  pl.BlockSpec((1, D), lambda b, i: (0, 0))],
        out_specs=tok(D),
        compiler_params=_params("parallel", "parallel"),
        name="out_proj",
    )(a, w, h, fw.reshape(1, D))


def _rope_tables(pos):
    half = HEAD_DIM // 2
    inv = ROPE_THETA ** (-jnp.arange(half, dtype=F32) / half)
    ang = pos.astype(F32)[:, None] * inv[None, :]
    cos = jnp.cos(ang)
    sin = jnp.sin(ang)
    return jnp.concatenate([cos, cos], axis=-1), jnp.concatenate([-sin, sin], axis=-1)


def _row_tile(n, pref):
    return pref if n % pref == 0 else n


def kernel(x_prompt, x_sample, cache_k, cache_v, cache_kidx, state_rec, norm_w, final_norm_w,
           att_w_in, att_w_out, rec_w_in, rec_w_out, rec_gnorm_w, rec_lb_logits):
    B, L, D = x_prompt.shape
    DB, T, _ = x_sample.shape
    past = cache_k.shape[2]
    depth = norm_w.shape[0]
    n_mix = 2

    att_in = att_w_in.shape[-1]
    att_pad = -(-att_in // LANES) * LANES
    w_att = [jnp.pad(att_w_in[j], ((0, 0), (0, att_pad - att_in))).astype(BF16) for j in range(att_w_in.shape[0])]
    w_att_out = att_w_out.astype(BF16)
    w_rec = rec_w_in.astype(BF16)
    w_rec_out = rec_w_out.astype(BF16)

    cos_p, sin_p = _rope_tables(jnp.arange(L, dtype=I32))
    cos_s, sin_s = _rope_tables(past + jnp.arange(T, dtype=I32))
    cos_s = jnp.tile(cos_s, (DB, 1))
    sin_s = jnp.tile(sin_s, (DB, 1))

    tk = 512
    lk_s = -(-(past + T) // tk) * tk

    h = x_prompt
    tm = _row_tile(L, 256)
    tq = _row_tile(L, 128)
    tl = _row_tile(L, 512)
    kp, vp, kip, sp = [], [], [], []
    for i in range(depth):
        j = i // n_mix
        last = i == depth - 1
        if i % n_mix == 0:
            q, qi, wi, z, k, v, ki, kb, vb, kib = _att_in_proj(h, norm_w[i], w_att[j], cos_p, sin_p, tm)
            a = _dsa_attention(q, qi, wi, z, kib, kb, vb, n_batch=B, n_qblk=L // tq, tq=tq,
                               tk=min(tk, L), per_batch_rows=True, q_pos_base=0, l_valid=L)
            h = _out_proj(a, w_att_out[j], h, final_norm_w, tm, last)
            kp.append(k.reshape(B, L, N_KV_HEADS, HEAD_DIM))
            vp.append(v.reshape(B, L, N_KV_HEADS, HEAD_DIM))
            kip.append(ki)
        else:
            p = _rec_in_proj(h, norm_w[i], w_rec[j], tm)
            s0 = jnp.zeros((B, REC_HEADS, REC_DK, REC_DV), F32)
            a, s = _rec_scan(p, rec_lb_logits, rec_gnorm_w[j], s0, layer=j, tl=tl, ch=min(CHUNK, L))
            h = _out_proj(a, w_rec_out[j], h, final_norm_w, tm, last)
            sp.append(s)
    y_prompt = h

    n_rows = DB * T
    h = x_sample.reshape(1, n_rows, D)
    ks_, vs_, kis_, ss_ = [], [], [], []
    for i in range(depth):
        j = i // n_mix
        last = i == depth - 1
        if i % n_mix == 0:
            q, qi, wi, z, k, v, ki, _, _, _ = _att_in_proj(h, norm_w[i], w_att[j], cos_s, sin_s, n_rows)
            pad = lk_s - past - T
            cat = lambda c, new: jnp.pad(jnp.concatenate([c.reshape(DB, past, -1), new.reshape(DB, T, -1)], axis=1),
                                         ((0, 0), (0, pad), (0, 0)))
            a = _dsa_attention(q, qi, wi, z, cat(cache_kidx[j], ki), cat(cache_k[j], k), cat(cache_v[j], v),
                               n_batch=DB, n_qblk=1, tq=T, tk=tk, per_batch_rows=False,
                               q_pos_base=past, l_valid=past + T)
            h = _out_proj(a, w_att_out[j], h, final_norm_w, n_rows, last)
            ks_.append(k.reshape(DB, T, N_KV_HEADS, HEAD_DIM))
            vs_.append(v.reshape(DB, T, N_KV_HEADS, HEAD_DIM))
            kis_.append(ki.reshape(DB, T, IDX_DIM))
        else:
            p = _rec_in_proj(h, norm_w[i], w_rec[j], n_rows)
            a, s = _rec_scan(p.reshape(DB, T, -1), rec_lb_logits, rec_gnorm_w[j], state_rec[j],
                             layer=j, tl=T, ch=T)
            h = _out_proj(a.reshape(1, n_rows, -1), w_rec_out[j], h, final_norm_w, n_rows, last)
            ss_.append(s)
    y_sample = h.reshape(DB, T, D)

    return (y_prompt, y_sample, jnp.stack(kp), jnp.stack(vp), jnp.stack(kip), jnp.stack(sp),
            jnp.stack(ks_), jnp.stack(vs_), jnp.stack(kis_), jnp.stack(ss_))
```

```python
import functools

import jax
import jax.numpy as jnp
from jax import lax
from jax.experimental import pallas as pl
from jax.experimental.pallas import tpu as pltpu

F32 = jnp.float32
BF16 = jnp.bfloat16
I32 = jnp.int32

EPS = 1e-6
CHUNK = 64
LANES = 128

N_HEADS = 8
HEAD_DIM = 128
N_KV_HEADS = 2
GROUP = N_HEADS // N_KV_HEADS
IDX_HEADS = 8
IDX_DIM = 128
MAX_SELECT = 256
ROPE_THETA = 10000.0
ATT_SCALE = HEAD_DIM ** -0.5
IDX_SCALE = (IDX_HEADS * IDX_DIM) ** -0.5
ATT_WIDTH = N_HEADS * HEAD_DIM
KV_WIDTH = N_KV_HEADS * HEAD_DIM
Q_PRESCALE = ATT_SCALE * 1.4426950408889634
QL = LANES

REC_HEADS = 8
REC_DK = 128
REC_DV = 128
SUB = 16

VMEM_LIMIT = 56 * 1024 * 1024
MASKED = -1e30
INT_MIN = -2 ** 31
INT_MAX = 2 ** 31 - 1


def _params(*sem):
    return pltpu.CompilerParams(dimension_semantics=sem, vmem_limit_bytes=VMEM_LIMIT)


def _sigmoid(x):
    return 1.0 / (1.0 + jnp.exp(-x))


def _rms_scale(x, w):
    ms = jnp.mean(x * x, axis=-1, keepdims=True)
    return x * lax.rsqrt(ms + EPS) * w


def _dot_nt(a, b):
    return lax.dot_general(a, b, (((1,), (1,)), ((), ())), preferred_element_type=F32)


def _dot_tn(a, b):
    return lax.dot_general(a, b, (((0,), (0,)), ((), ())), preferred_element_type=F32)


def _att_in_kernel(x_ref, nw_ref, w_ref, wt_ref, cos_ref, sin_ref,
                   q_ref, qi_ref, wi_ref, z_ref, k_ref, v_ref, ki_ref, kb_ref, vb_ref, kib_ref):
    xn = _rms_scale(x_ref[...], nw_ref[...]).astype(BF16)
    cos = cos_ref[...]
    sin = sin_ref[...]

    def rope(t):
        return t * cos + pltpu.roll(t, HEAD_DIM // 2, 1) * sin

    def proj(c0, c1):
        return jnp.dot(xn, w_ref[:, c0:c1], preferred_element_type=F32)

    c = 0
    pq = proj(c, c + ATT_WIDTH)
    for h in range(N_HEADS):
        q_ref[h] = (rope(pq[:, h * HEAD_DIM:(h + 1) * HEAD_DIM]) * Q_PRESCALE).astype(BF16)
    c += ATT_WIDTH
    pkv = proj(c, c + 2 * KV_WIDTH)
    for g in range(N_KV_HEADS):
        kg = rope(pkv[:, g * HEAD_DIM:(g + 1) * HEAD_DIM])
        k_ref[:, g * HEAD_DIM:(g + 1) * HEAD_DIM] = kg
        kb_ref[:, g * HEAD_DIM:(g + 1) * HEAD_DIM] = kg.astype(BF16)
    vv = pkv[:, KV_WIDTH:]
    v_ref[...] = vv
    vb_ref[...] = vv.astype(BF16)
    c += 2 * KV_WIDTH
    z_ref[...] = proj(c, c + ATT_WIDTH)
    c += ATT_WIDTH
    pqi = proj(c, c + IDX_HEADS * IDX_DIM)
    for h in range(IDX_HEADS):
        qi_ref[h] = rope(pqi[:, h * IDX_DIM:(h + 1) * IDX_DIM]).astype(BF16)
    c += IDX_HEADS * IDX_DIM
    ki = rope(proj(c, c + IDX_DIM))
    ki_ref[...] = ki
    kib_ref[...] = ki.astype(BF16)
    wi_ref[...] = _dot_nt(wt_ref[...], xn)[:IDX_HEADS] * IDX_SCALE


def _att_in_proj(x, nw, w_main, w_wi_t, cos, sin, tm):
    B, L, D = x.shape
    grid = (B, L // tm)
    tok = lambda width: pl.BlockSpec((None, tm, width), lambda b, i: (b, i, 0))
    hm = pl.BlockSpec((None, N_HEADS, tm, HEAD_DIM), lambda b, i: (b, 0, i, 0))
    out_shape = (
        jax.ShapeDtypeStruct((B, N_HEADS, L, HEAD_DIM), BF16),
        jax.ShapeDtypeStruct((B, IDX_HEADS, L, IDX_DIM), BF16),
        jax.ShapeDtypeStruct((B, IDX_HEADS, L), F32),
        jax.ShapeDtypeStruct((B, L, ATT_WIDTH), F32),
        jax.ShapeDtypeStruct((B, L, KV_WIDTH), F32),
        jax.ShapeDtypeStruct((B, L, KV_WIDTH), F32),
        jax.ShapeDtypeStruct((B, L, IDX_DIM), F32),
        jax.ShapeDtypeStruct((B, L, KV_WIDTH), BF16),
        jax.ShapeDtypeStruct((B, L, KV_WIDTH), BF16),
        jax.ShapeDtypeStruct((B, L, IDX_DIM), BF16),
    )
    wi_spec = pl.BlockSpec((None, IDX_HEADS, tm), lambda b, i: (b, 0, i))
    out_specs = (hm, hm, wi_spec, tok(ATT_WIDTH), tok(KV_WIDTH), tok(KV_WIDTH), tok(IDX_DIM),
                 tok(KV_WIDTH), tok(KV_WIDTH), tok(IDX_DIM))
    return pl.pallas_call(
        _att_in_kernel,
        out_shape=out_shape,
        grid=grid,
        in_specs=[
            tok(D),
            pl.BlockSpec((1, D), lambda b, i: (0, 0)),
            pl.BlockSpec(w_main.shape, lambda b, i: (0, 0)),
            pl.BlockSpec(w_wi_t.shape, lambda b, i: (0, 0)),
            pl.BlockSpec((tm, HEAD_DIM), lambda b, i: (i, 0)),
            pl.BlockSpec((tm, HEAD_DIM), lambda b, i: (i, 0)),
        ],
        out_specs=out_specs,
        compiler_params=_params("parallel", "parallel"),
        name="att_in_proj",
    )(x, nw.reshape(1, D), w_main, w_wi_t, cos, sin)


def _dsa_kernel(q_ref, qi_ref, wi_ref, z_ref, ki_ref, k_ref, v_ref, o_ref,
                key_ref, jfin_ref, m_ref, acc_ref, qx_ref,
                *, tq, tk, q_pos_base, l_valid, n_sel, idx_bits):
    i = pl.program_id(1)
    q0 = q_pos_base + i * tq
    q_pos = q0 + lax.broadcasted_iota(I32, (1, QL), 1)
    q_lim = jnp.minimum((q_pos // CHUNK + 1) * CHUNK, l_valid)
    last_lim = jnp.minimum(((q0 + tq - 1) // CHUNK + 1) * CHUNK, l_valid)
    nk = (last_lim + tk - 1) // tk
    k_iota = lax.broadcasted_iota(I32, (tk, 1), 0)

    qi = qi_ref[...].reshape(IDX_HEADS * QL, IDX_DIM)
    wi = wi_ref[...]

    def for_tiles(body):
        def pair(p, carry):
            body(2 * p)
            body(2 * p + 1)
            return carry
        lax.fori_loop(0, nk // 2, pair, 0)

        @pl.when(nk % 2 == 1)
        def _():
            body(nk - 1)

    def score_tile(j):
        r = pl.multiple_of(j * tk, tk)
        kt = ki_ref[pl.ds(r, tk), :].astype(BF16)
        s = _dot_nt(kt, qi)
        acc = jnp.zeros((tk, QL), F32)
        for h in range(IDX_HEADS):
            acc = acc + wi[h:h + 1, :] * jnp.maximum(s[:, h * QL:(h + 1) * QL], 0.0)
        adm = (j * tk + k_iota) < q_lim
        sc = jnp.where(adm, acc, -jnp.inf)
        sc = jnp.where(sc == 0.0, 0.0, sc)
        bits = lax.bitcast_convert_type(sc, I32)
        key_ref[j] = bits ^ ((bits >> 31) & INT_MAX)

    for_tiles(score_tile)

    def count_keys(pred):
        def body(j, cnt):
            hit = pred(key_ref[j], j * tk).astype(I32)
            return cnt + jnp.sum(hit.reshape(tk // 8, 8, QL), axis=0)
        cnt = lax.fori_loop(0, nk, body, jnp.zeros((8, QL), I32))
        return jnp.sum(cnt, axis=0, keepdims=True)

    def search_bit(it, carry):
        t_biased, c_ge = carry
        cand = t_biased | lax.shift_left(jnp.int32(1), 31 - it)
        cand_s = cand ^ INT_MIN
        c = count_keys(lambda kk, base: kk >= cand_s)
        ok = c >= n_sel
        return jnp.where(ok, cand, t_biased), jnp.where(ok, c, c_ge)

    t_biased, c_ge = lax.fori_loop(
        0, 32, search_bit, (jnp.zeros((1, QL), I32), jnp.zeros((1, QL), I32) + nk * tk))
    thr = t_biased ^ INT_MIN

    jfin_ref[...] = jnp.full((1, QL), INT_MAX, I32)

    @pl.when(jnp.max(c_ge) > n_sel)
    def _():
        c_gt = count_keys(lambda kk, base: kk > thr)
        need = n_sel - c_gt

        def idx_bit(it, jlo):
            cand = jlo | lax.shift_left(jnp.int32(1), idx_bits - 1 - it)
            c = count_keys(lambda kk, base: (kk == thr) & ((k_iota + base) < cand))
            return jnp.where(c < need, cand, jlo)

        jlo = lax.fori_loop(0, idx_bits, idx_bit, jnp.zeros((1, QL), I32))
        jfin_ref[...] = jnp.where(c_ge > n_sel, jlo, INT_MAX)

    jfin = jfin_ref[...]
    m_ref[...] = jnp.full(m_ref.shape, -jnp.inf, F32)
    acc_ref[...] = jnp.zeros(acc_ref.shape, F32)
    ones = jnp.ones((tk, HEAD_DIM), BF16)
    oh_row = lax.broadcasted_iota(I32, (GROUP * tq, QL), 0) & (tq - 1)
    onehot = (oh_row == lax.broadcasted_iota(I32, (GROUP * tq, QL), 1)).astype(BF16)
    for g in range(N_KV_HEADS):
        qg = q_ref[g * GROUP:(g + 1) * GROUP].reshape(GROUP * tq, HEAD_DIM)
        qx_ref[g] = jnp.concatenate([qg, onehot], axis=1)

    def attend_tile(j):
        r = pl.multiple_of(j * tk, tk)
        kk = key_ref[j]
        kpos = j * tk + k_iota
        sel = (kk > thr) | ((kk == thr) & (kpos <= jfin))
        sel = sel & (kpos < q_lim)
        bias = jnp.where(sel, 0.0, MASKED).astype(BF16)
        kt = k_ref[pl.ds(r, tk), :].astype(BF16)
        vt = v_ref[pl.ds(r, tk), :].astype(BF16)
        for g in range(N_KV_HEADS):
            kx = jnp.concatenate([kt[:, g * HEAD_DIM:(g + 1) * HEAD_DIM], bias], axis=1)
            s = _dot_nt(qx_ref[g], kx)
            m_prev = m_ref[g]
            m_new = jnp.maximum(m_prev, jnp.max(s, axis=-1, keepdims=True))
            alpha = jnp.exp2(m_prev - m_new)
            p = jnp.exp2(s - jnp.tile(m_new, (1, tk // LANES))).astype(BF16)
            vx = jnp.concatenate([vt[:, g * HEAD_DIM:(g + 1) * HEAD_DIM], ones], axis=1)
            acc_ref[g] = jnp.tile(alpha, (1, 2)) * acc_ref[g] + jnp.dot(p, vx, preferred_element_type=F32)
            m_ref[g] = m_new

    for_tiles(attend_tile)

    for g in range(N_KV_HEADS):
        acc = acc_ref[g]
        o = acc[:, :HEAD_DIM] / acc[:, HEAD_DIM:]
        for u in range(GROUP):
            h = g * GROUP + u
            z = z_ref[:, h * HEAD_DIM:(h + 1) * HEAD_DIM]
            gated = o[u * tq:(u + 1) * tq] * (z * _sigmoid(z))
            o_ref[:, h * HEAD_DIM:(h + 1) * HEAD_DIM] = gated.astype(BF16)


def _dsa_attention(q_hm, qi_hm, wi_hm, z, ki, k, v, *, n_batch, n_qblk, tq, tk, per_batch_rows,
                   q_pos_base, l_valid):
    lk = ki.shape[1]
    assert tq & (tq - 1) == 0 and tq <= QL
    n_sel = min(MAX_SELECT, l_valid // 4)
    idx_bits = max(1, (lk - 1).bit_length())
    if per_batch_rows:
        qmap = lambda b, i: (b, 0, i, 0)
        rmap = lambda b, i: (b, i, 0)
        imap = qmap
        wmap = lambda b, i: (b, 0, i)
    else:
        qmap = lambda b, i: (0, 0, b, 0)
        rmap = lambda b, i: (0, b, 0)
        imap = lambda b, i: (b, 0, 0, 0)
        wmap = lambda b, i: (b, 0, 0)
    kmap = lambda b, i: (b, 0, 0)
    kern = functools.partial(_dsa_kernel, tq=tq, tk=tk, q_pos_base=q_pos_base, l_valid=l_valid,
                             n_sel=n_sel, idx_bits=idx_bits)
    return pl.pallas_call(
        kern,
        out_shape=jax.ShapeDtypeStruct(z.shape, BF16),
        grid=(n_batch, n_qblk),
        in_specs=[
            pl.BlockSpec((None, N_HEADS, tq, HEAD_DIM), qmap),
            pl.BlockSpec((None, IDX_HEADS, QL, IDX_DIM), imap),
            pl.BlockSpec((None, IDX_HEADS, QL), wmap),
            pl.BlockSpec((None, tq, ATT_WIDTH), rmap),
            pl.BlockSpec((None, lk, IDX_DIM), kmap),
            pl.BlockSpec((None, lk, KV_WIDTH), kmap),
            pl.BlockSpec((None, lk, KV_WIDTH), kmap),
        ],
        out_specs=pl.BlockSpec((None, tq, ATT_WIDTH), rmap),
        scratch_shapes=[
            pltpu.VMEM((lk // tk, tk, QL), I32),
            pltpu.VMEM((1, QL), I32),
            pltpu.VMEM((N_KV_HEADS, GROUP * tq, LANES), F32),
            pltpu.VMEM((N_KV_HEADS, GROUP * tq, 2 * HEAD_DIM), F32),
            pltpu.VMEM((N_KV_HEADS, GROUP * tq, HEAD_DIM + QL), BF16),
        ],
        compiler_params=_params("parallel", "arbitrary"),
        name="dsa_attention",
    )(q_hm, qi_hm, wi_hm, z, ki, k, v)


def _rec_in_kernel(x_ref, nw_ref, w_ref, p_ref, *, col_chunk):
    xn = _rms_scale(x_ref[...], nw_ref[...]).astype(BF16)
    for c in range(0, w_ref.shape[1], col_chunk):
        p_ref[:, c:c + col_chunk] = jnp.dot(xn, w_ref[:, c:c + col_chunk], preferred_element_type=F32)


def _rec_in_proj(x, nw, w, tm):
    B, L, D = x.shape
    n_out = w.shape[1]
    return pl.pallas_call(
        functools.partial(_rec_in_kernel, col_chunk=1024),
        out_shape=jax.ShapeDtypeStruct((B, L, n_out), F32),
        grid=(B, L // tm),
        in_specs=[
            pl.BlockSpec((None, tm, D), lambda b, i: (b, i, 0)),
            pl.BlockSpec((1, D), lambda b, i: (0, 0)),
            pl.BlockSpec(w.shape, lambda b, i: (0, 0)),
        ],
        out_specs=pl.BlockSpec((None, tm, n_out), lambda b, i: (b, i, 0)),
        compiler_params=_params("parallel", "parallel"),
        name="rec_in_proj",
    )(x, nw.reshape(1, D), w)


def _rec_kernel(qp_ref, fp_ref, v_ref, z_ref, lbl_ref, gn_ref, s0_ref, o_ref, sout_ref,
                st_ref, k_all, b_all, v_all, *, layer, ch, n_ch):
    t = pl.program_id(2)

    @pl.when(t == 0)
    def _():
        st_ref[...] = s0_ref[...].T

    logits = lbl_ref[...]
    e = jnp.exp(logits - jnp.max(logits, axis=0, keepdims=True))
    p_lb = e / jnp.sum(e, axis=0, keepdims=True)
    lb = jnp.zeros((1, REC_DK), F32)
    for r in range(1, layer + 1):
        lb = lb + p_lb[r:r + 1]
    gn = gn_ref[...]

    tri = (lax.broadcasted_iota(I32, (ch, ch), 0) >= lax.broadcasted_iota(I32, (ch, ch), 1)).astype(F32)
    sub_row = lax.broadcasted_iota(I32, (SUB, 1), 0)
    ch_row = lax.broadcasted_iota(I32, (ch, 1), 0)

    def chunk(c, st, slot):
        r = pl.multiple_of(c * ch, ch)
        qp = qp_ref[pl.ds(r, ch), :]
        f = lb + (1.0 - lb) * _sigmoid(fp_ref[pl.ds(r, ch), :])
        q = qp * _sigmoid(qp)
        g = jnp.log(f)
        kk = 1.0 - f
        v = v_ref[pl.ds(r, ch), :]
        b = jnp.dot(tri, g, preferred_element_type=F32, precision=lax.Precision.HIGHEST)
        k_s, b_s, v_s = k_all.at[slot], b_all.at[slot], v_all.at[slot]
        k_s[...] = kk
        b_s[...] = b
        v_s[...] = v
        vb = v.astype(BF16)

        o_inter = _dot_nt((q * jnp.exp(b)).astype(BF16), st.astype(BF16))

        for blk in range(ch // SUB):
            r0 = blk * SUB
            q_i = q[r0:r0 + SUB]
            b_i = b[r0:r0 + SUB]
            o_i = o_inter[r0:r0 + SUB]
            for s in range(SUB):
                d = jnp.minimum(b_i - b_s[r0 + s:r0 + s + 1, :], 0.0)
                w = jnp.sum(q_i * k_s[r0 + s:r0 + s + 1, :] * jnp.exp(d), axis=-1, keepdims=True)
                w = jnp.where(sub_row >= s, w, 0.0)
                o_i = o_i + w * v_s[r0 + s:r0 + s + 1, :]
            if blk > 0:
                c_i = b[r0 - 1:r0]
                a = (q_i * jnp.exp(b_i - c_i)).astype(BF16)
                kd = jnp.where(ch_row < r0, kk * jnp.exp(jnp.minimum(c_i - b, 0.0)), 0.0).astype(BF16)
                pm = _dot_nt(a, kd)
                o_i = o_i + jnp.dot(pm.astype(BF16), vb, preferred_element_type=F32)
            o_n = o_i * lax.rsqrt(jnp.mean(o_i * o_i, axis=-1, keepdims=True) + EPS) * gn
            z = z_ref[pl.ds(r + r0, SUB), :]
            o_ref[pl.ds(r + r0, SUB), :] = (o_n * (z * _sigmoid(z))).astype(o_ref.dtype)

        b_last = b[ch - 1:ch]
        kd = (kk * jnp.exp(b_last - b)).astype(BF16)
        return st * jnp.exp(b_last) + _dot_tn(vb, kd)

    st = st_ref[...]
    if n_ch % 2 == 0:
        def pair(c2, st):
            return chunk(2 * c2 + 1, chunk(2 * c2, st, 0), 1)
        st = lax.fori_loop(0, n_ch // 2, pair, st)
    else:
        st = lax.fori_loop(0, n_ch, lambda c, st: chunk(c, st, 0), st)
    st_ref[...] = st

    @pl.when(t == pl.num_programs(2) - 1)
    def _():
        sout_ref[...] = st_ref[...].T


def _rec_scan(p, lb_logits, gnorm, s0, *, layer, tl, ch):
    B, L, _ = p.shape
    H = REC_HEADS
    col = lambda off: pl.BlockSpec((None, tl, REC_DK), lambda b, h, t: (b, t, off + h))
    n_rec = lb_logits.shape[0]
    return pl.pallas_call(
        functools.partial(_rec_kernel, layer=layer, ch=ch, n_ch=tl // ch),
        out_shape=(jax.ShapeDtypeStruct((B, L, H * REC_DV), BF16),
                   jax.ShapeDtypeStruct(s0.shape, F32)),
        grid=(B, H, L // tl),
        in_specs=[
            col(0), col(H), col(2 * H), col(3 * H),
            pl.BlockSpec((n_rec, REC_DK), lambda b, h, t: (0, h)),
            pl.BlockSpec((1, REC_DV), lambda b, h, t: (0, h)),
            pl.BlockSpec((None, None, REC_DK, REC_DV), lambda b, h, t: (b, h, 0, 0)),
        ],
        out_specs=(pl.BlockSpec((None, tl, REC_DV), lambda b, h, t: (b, t, h)),
                   pl.BlockSpec((None, None, REC_DK, REC_DV), lambda b, h, t: (b, h, 0, 0))),
        scratch_shapes=[pltpu.VMEM((REC_DV, REC_DK), F32)] + [pltpu.VMEM((2, ch, REC_DK), F32)] * 3,
        compiler_params=_params("parallel", "parallel", "arbitrary"),
        name="rec_scan",
    )(p, p, p, p, lb_logits, gnorm.reshape(1, H * REC_DV), s0)


def _out_kernel(a_ref, w_ref, h_ref, fw_ref, o_ref, *, final_norm):
    h = h_ref[...] + jnp.dot(a_ref[...].astype(BF16), w_ref[...], preferred_element_type=F32)
    if final_norm:
        h = _rms_scale(h, fw_ref[...])
    o_ref[...] = h


def _out_proj(a, w, h, fw, tm, final_norm):
    B, L, D = h.shape
    tok = lambda width: pl.BlockSpec((None, tm, width), lambda b, i: (b, i, 0))
    return pl.pallas_call(
        functools.partial(_out_kernel, final_norm=final_norm),
        out_shape=jax.ShapeDtypeStruct(h.shape, F32),
        grid=(B, L // tm),
        in_specs=[tok(a.shape[-1]), pl.BlockSpec(w.shape, lambda b, i: (0, 0)), tok(D),
                  pl.BlockSpec((1, D), lambda b, i: (0, 0))],
        out_specs=tok(D),
        compiler_params=_params("parallel", "parallel"),
        name="out_proj",
    )(a, w, h, fw.reshape(1, D))


def _rope_tables(pos):
    half = HEAD_DIM // 2
    inv = ROPE_THETA ** (-jnp.arange(half, dtype=F32) / half)
    ang = pos.astype(F32)[:, None] * inv[None, :]
    cos = jnp.cos(ang)
    sin = jnp.sin(ang)
    return jnp.concatenate([cos, cos], axis=-1), jnp.concatenate([-sin, sin], axis=-1)


def _row_tile(n, pref):
    return pref if n % pref == 0 else n


def kernel(x_prompt, x_sample, cache_k, cache_v, cache_kidx, state_rec, norm_w, final_norm_w,
           att_w_in, att_w_out, rec_w_in, rec_w_out, rec_gnorm_w, rec_lb_logits):
    B, L, D = x_prompt.shape
    DB, T, _ = x_sample.shape
    past = cache_k.shape[2]
    depth = norm_w.shape[0]
    n_mix = 2

    n_main = att_w_in.shape[-1] - IDX_HEADS
    w_att = att_w_in[:, :, :n_main].astype(BF16)
    w_wi_t = jnp.pad(jnp.swapaxes(att_w_in[:, :, n_main:], 1, 2), ((0, 0), (0, 16 - IDX_HEADS), (0, 0))).astype(BF16)
    w_att_out = att_w_out.astype(BF16)
    w_rec = rec_w_in.astype(BF16)
    w_rec_out = rec_w_out.astype(BF16)

    cos_p, sin_p = _rope_tables(jnp.arange(L, dtype=I32))
    cos_s, sin_s = _rope_tables(past + jnp.arange(T, dtype=I32))
    cos_s = jnp.tile(cos_s, (DB, 1))
    sin_s = jnp.tile(sin_s, (DB, 1))

    tk = 512
    lk_s = -(-(past + T) // tk) * tk

    h = x_prompt
    tm = _row_tile(L, 256)
    tq = _row_tile(L, 128)
    tl = _row_tile(L, 512)
    kp, vp, kip, sp = [], [], [], []
    for i in range(depth):
        j = i // n_mix
        last = i == depth - 1
        if i % n_mix == 0:
            q, qi, wi, z, k, v, ki, kb, vb, kib = _att_in_proj(h, norm_w[i], w_att[j], w_wi_t[j], cos_p, sin_p, tm)
            a = _dsa_attention(q, qi, wi, z, kib, kb, vb, n_batch=B, n_qblk=L // tq, tq=tq,
                               tk=min(tk, L), per_batch_rows=True, q_pos_base=0, l_valid=L)
            h = _out_proj(a, w_att_out[j], h, final_norm_w, tm, last)
            kp.append(k.reshape(B, L, N_KV_HEADS, HEAD_DIM))
            vp.append(v.reshape(B, L, N_KV_HEADS, HEAD_DIM))
            kip.append(ki)
        else:
            p = _rec_in_proj(h, norm_w[i], w_rec[j], tm)
            s0 = jnp.zeros((B, REC_HEADS, REC_DK, REC_DV), F32)
            a, s = _rec_scan(p, rec_lb_logits, rec_gnorm_w[j], s0, layer=j, tl=tl, ch=min(CHUNK, L))
            h = _out_proj(a, w_rec_out[j], h, final_norm_w, tm, last)
            sp.append(s)
    y_prompt = h

    n_rows = DB * T
    h = x_sample.reshape(1, n_rows, D)
    ks_, vs_, kis_, ss_ = [], [], [], []
    for i in range(depth):
        j = i // n_mix
        last = i == depth - 1
        if i % n_mix == 0:
            q, qi, wi, z, k, v, ki, _, _, _ = _att_in_proj(h, norm_w[i], w_att[j], w_wi_t[j], cos_s, sin_s, n_rows)
            qi = jnp.pad(jnp.swapaxes(qi.reshape(IDX_HEADS, DB, T, IDX_DIM), 0, 1),
                         ((0, 0), (0, 0), (0, QL - T), (0, 0)))
            wi = jnp.pad(jnp.swapaxes(wi.reshape(IDX_HEADS, DB, T), 0, 1), ((0, 0), (0, 0), (0, QL - T)))
            pad = lk_s - past - T
            cat = lambda c, new: jnp.pad(jnp.concatenate([c.reshape(DB, past, -1), new.reshape(DB, T, -1)], axis=1),
                                         ((0, 0), (0, pad), (0, 0)))
            a = _dsa_attention(q, qi, wi, z, cat(cache_kidx[j], ki), cat(cache_k[j], k), cat(cache_v[j], v),
                               n_batch=DB, n_qblk=1, tq=T, tk=tk, per_batch_rows=False,
                               q_pos_base=past, l_valid=past + T)
            h = _out_proj(a, w_att_out[j], h, final_norm_w, n_rows, last)
            ks_.append(k.reshape(DB, T, N_KV_HEADS, HEAD_DIM))
            vs_.append(v.reshape(DB, T, N_KV_HEADS, HEAD_DIM))
            kis_.append(ki.reshape(DB, T, IDX_DIM))
        else:
            p = _rec_in_proj(h, norm_w[i], w_rec[j], n_rows)
            a, s = _rec_scan(p.reshape(DB, T, -1), rec_lb_logits, rec_gnorm_w[j], state_rec[j],
                             layer=j, tl=T, ch=T)
            h = _out_proj(a.reshape(1, n_rows, -1), w_rec_out[j], h, final_norm_w, n_rows, last)
            ss_.append(s)
    y_sample = h.reshape(DB, T, D)

    return (y_prompt, y_sample, jnp.stack(kp), jnp.stack(vp), jnp.stack(kip), jnp.stack(sp),
            jnp.stack(ks_), jnp.stack(vs_), jnp.stack(kis_), jnp.stack(ss_))
```

```python
import functools

import jax
import jax.numpy as jnp
from jax import lax
from jax.experimental import pallas as pl
from jax.experimental.pallas import tpu as pltpu

F32 = jnp.float32
BF16 = jnp.bfloat16
I32 = jnp.int32
I16 = jnp.int16

EPS = 1e-6
CHUNK = 64
LANES = 128

N_HEADS = 8
HEAD_DIM = 128
N_KV_HEADS = 2
GROUP = N_HEADS // N_KV_HEADS
IDX_HEADS = 8
IDX_DIM = 128
MAX_SELECT = 256
ROPE_THETA = 10000.0
ATT_SCALE = HEAD_DIM ** -0.5
IDX_SCALE = (IDX_HEADS * IDX_DIM) ** -0.5
ATT_WIDTH = N_HEADS * HEAD_DIM
KV_WIDTH = N_KV_HEADS * HEAD_DIM
Q_PRESCALE = ATT_SCALE * 1.4426950408889634
QL = LANES

REC_HEADS = 8
REC_DK = 128
REC_DV = 128
SUB = 16
REC_HPS = 2

VMEM_LIMIT = 56 * 1024 * 1024
MASKED = -1e30
INT_MIN = -2 ** 31
INT_MAX = 2 ** 31 - 1


def _params(*sem):
    return pltpu.CompilerParams(dimension_semantics=sem, vmem_limit_bytes=VMEM_LIMIT)


def _sigmoid(x):
    return 1.0 / (1.0 + jnp.exp(-x))


def _rms_scale(x, w):
    ms = jnp.mean(x * x, axis=-1, keepdims=True)
    return x * lax.rsqrt(ms + EPS) * w


def _dot_nt(a, b):
    return lax.dot_general(a, b, (((1,), (1,)), ((), ())), preferred_element_type=F32)


def _dot_tn(a, b):
    return lax.dot_general(a, b, (((0,), (0,)), ((), ())), preferred_element_type=F32)


def _att_in_kernel(*refs):
    x_ref, nw_ref, w_ref, wt_ref, cos_ref, sin_ref = refs[:6]
    q_ref, qi_ref, wi_ref, z_ref, k_ref, v_ref, ki_ref, kb_ref, vb_ref, kib_ref = refs[-10:]
    xn = _rms_scale(x_ref[...], nw_ref[...]).astype(BF16)
    cos = cos_ref[...]
    sin = sin_ref[...]

    def rope(t):
        return t * cos + pltpu.roll(t, HEAD_DIM // 2, 1) * sin

    def proj(c0, c1):
        return jnp.dot(xn, w_ref[:, c0:c1], preferred_element_type=F32)

    c = 0
    pq = proj(c, c + ATT_WIDTH)
    for h in range(N_HEADS):
        q_ref[h] = (rope(pq[:, h * HEAD_DIM:(h + 1) * HEAD_DIM]) * Q_PRESCALE).astype(BF16)
    c += ATT_WIDTH
    pkv = proj(c, c + 2 * KV_WIDTH)
    for g in range(N_KV_HEADS):
        kg = rope(pkv[:, g * HEAD_DIM:(g + 1) * HEAD_DIM])
        k_ref[:, g * HEAD_DIM:(g + 1) * HEAD_DIM] = kg
        kb_ref[:, g * HEAD_DIM:(g + 1) * HEAD_DIM] = kg.astype(BF16)
    vv = pkv[:, KV_WIDTH:]
    v_ref[...] = vv
    vb_ref[...] = vv.astype(BF16)
    c += 2 * KV_WIDTH
    z_ref[...] = proj(c, c + ATT_WIDTH)
    c += ATT_WIDTH
    pqi = proj(c, c + IDX_HEADS * IDX_DIM)
    for h in range(IDX_HEADS):
        qi_ref[h] = rope(pqi[:, h * IDX_DIM:(h + 1) * IDX_DIM]).astype(BF16)
    c += IDX_HEADS * IDX_DIM
    ki = rope(proj(c, c + IDX_DIM))
    ki_ref[...] = ki
    kib_ref[...] = ki.astype(BF16)
    wi_ref[...] = _dot_nt(wt_ref[...], xn)[:IDX_HEADS] * IDX_SCALE


def _att_in_proj(x, nw, w_main, w_wi_t, cos, sin, tm, *, layer, stacks):
    n_layers = stacks[0].shape[0]
    B, L, D = x.shape
    grid = (B, L // tm)
    tok = lambda width: pl.BlockSpec((None, tm, width), lambda b, i: (b, i, 0))
    hm = pl.BlockSpec((None, N_HEADS, tm, HEAD_DIM), lambda b, i: (b, 0, i, 0))
    slab = lambda width: pl.BlockSpec((None, None, tm, width), lambda b, i: (layer, b, i, 0))
    stacked = lambda width: jax.ShapeDtypeStruct((n_layers, B, L, width), F32)
    out_shape = (
        jax.ShapeDtypeStruct((B, N_HEADS, L, HEAD_DIM), BF16),
        jax.ShapeDtypeStruct((B, IDX_HEADS, L, IDX_DIM), BF16),
        jax.ShapeDtypeStruct((B, IDX_HEADS, L), F32),
        jax.ShapeDtypeStruct((B, L, ATT_WIDTH), F32),
        stacked(KV_WIDTH),
        stacked(KV_WIDTH),
        stacked(IDX_DIM),
        jax.ShapeDtypeStruct((B, L, KV_WIDTH), BF16),
        jax.ShapeDtypeStruct((B, L, KV_WIDTH), BF16),
        jax.ShapeDtypeStruct((B, L, IDX_DIM), BF16),
    )
    wi_spec = pl.BlockSpec((None, IDX_HEADS, tm), lambda b, i: (b, 0, i))
    out_specs = (hm, hm, wi_spec, tok(ATT_WIDTH), slab(KV_WIDTH), slab(KV_WIDTH), slab(IDX_DIM),
                 tok(KV_WIDTH), tok(KV_WIDTH), tok(IDX_DIM))
    in_specs = [
        tok(D),
        pl.BlockSpec((1, D), lambda b, i: (0, 0)),
        pl.BlockSpec(w_main.shape, lambda b, i: (0, 0)),
        pl.BlockSpec(w_wi_t.shape, lambda b, i: (0, 0)),
        pl.BlockSpec((tm, HEAD_DIM), lambda b, i: (i, 0)),
        pl.BlockSpec((tm, HEAD_DIM), lambda b, i: (i, 0)),
    ]
    args = [x, nw.reshape(1, D), w_main, w_wi_t, cos, sin]
    in_specs += [pl.BlockSpec(memory_space=pl.ANY)] * 3
    aliases = {len(args) + n: 4 + n for n in range(3)}
    args += list(stacks)
    return pl.pallas_call(
        _att_in_kernel,
        out_shape=out_shape,
        grid=grid,
        in_specs=in_specs,
        out_specs=out_specs,
        input_output_aliases=aliases,
        compiler_params=_params("parallel", "parallel"),
        name="att_in_proj",
    )(*args)


def _dsa_kernel(*refs, tq, tk, q_pos_base, l_valid, n_sel, idx_bits, n_cache_tiles):
    if n_cache_tiles is None:
        q_ref, qi_ref, wi_ref, z_ref, ki_ref, k_ref, v_ref, o_ref = refs[:8]
    else:
        q_ref, qi_ref, wi_ref, z_ref, ki_ref, k_ref, v_ref, kin_ref, kn_ref, vn_ref, o_ref = refs[:11]
    key_ref, half_ref, jfin_ref, m_ref, acc_ref, qx_ref = refs[-6:]
    i = pl.program_id(1)
    q0 = q_pos_base + i * tq
    q_pos = q0 + lax.broadcasted_iota(I32, (1, QL), 1)
    q_lim = jnp.minimum((q_pos // CHUNK + 1) * CHUNK, l_valid)
    last_lim = jnp.minimum(((q0 + tq - 1) // CHUNK + 1) * CHUNK, l_valid)
    nk = (last_lim + tk - 1) // tk
    k_iota = lax.broadcasted_iota(I32, (tk, 1), 0)

    qi = qi_ref[...].reshape(IDX_HEADS * QL, IDX_DIM)
    wi = wi_ref[...]

    nk_main = nk if n_cache_tiles is None else n_cache_tiles

    def for_tiles(body):
        def pair(p, carry):
            body(2 * p)
            body(2 * p + 1)
            return carry
        lax.fori_loop(0, nk_main // 2, pair, 0)
        if isinstance(nk_main, int):
            if nk_main % 2 == 1:
                body(nk_main - 1)
        else:
            pl.when(nk_main % 2 == 1)(lambda: body(nk_main - 1))

    def rows(ref, j):
        return ref[pl.ds(pl.multiple_of(j * tk, tk), tk), :]

    def score_tile(j, kt):
        s = _dot_nt(kt.astype(BF16), qi)
        acc = jnp.zeros((tk, QL), F32)
        for h in range(IDX_HEADS):
            acc = acc + wi[h:h + 1, :] * jnp.maximum(s[:, h * QL:(h + 1) * QL], 0.0)
        adm = (j * tk + k_iota) < q_lim
        sc = jnp.where(adm, acc, -jnp.inf)
        sc = jnp.where(sc == 0.0, 0.0, sc)
        bits = lax.bitcast_convert_type(sc, I32)
        key = bits ^ ((bits >> 31) & INT_MAX)
        key_ref[j] = key
        half_ref[j] = (key >> 16).astype(I16)

    for_tiles(lambda j: score_tile(j, rows(ki_ref, j)))
    if n_cache_tiles is not None:
        score_tile(n_cache_tiles, kin_ref[...])

    def count_keys(pred):
        def body(j, cnt):
            hit = pred(key_ref[j], j * tk).astype(I32)
            return cnt + jnp.sum(hit.reshape(tk // 8, 8, QL), axis=0)
        cnt = lax.fori_loop(0, nk, body, jnp.zeros((8, QL), I32))
        return jnp.sum(cnt, axis=0, keepdims=True)

    def count_half_ge(cand16):
        cand_b = jnp.broadcast_to(cand16.astype(I16), (16, QL))
        def body(j, cnt):
            parts = [(half_ref[j, c * 16:(c + 1) * 16, :] >= cand_b).astype(I16) for c in range(tk // 16)]
            while len(parts) > 1:
                parts = [a + b for a, b in zip(parts[::2], parts[1::2])]
            return cnt + parts[0]
        cnt = lax.fori_loop(0, nk, body, jnp.zeros((16, QL), I16))
        return jnp.sum(cnt.astype(I32), axis=0, keepdims=True)

    def search_half(c_init):
        def bit(it, carry):
            t_biased, c_ge = carry
            cand = t_biased | lax.shift_left(jnp.int32(1), 15 - it)
            c = count_half_ge(cand - 32768)
            ok = c >= n_sel
            return jnp.where(ok, cand, t_biased), jnp.where(ok, c, c_ge)
        return lax.fori_loop(0, 16, bit, (jnp.zeros((1, QL), I32), c_init))

    t_hi, c_ge = search_half(jnp.zeros((1, QL), I32) + nk * tk)
    t_hi_s = t_hi - 32768

    def low_tile(j, carry):
        key = key_ref[j]
        hi = key >> 16
        lo = (key & 0xFFFF) - 32768
        half_ref[j] = jnp.where(hi > t_hi_s, 32767, jnp.where(hi < t_hi_s, -32768, lo)).astype(I16)
        return carry

    lax.fori_loop(0, nk, low_tile, 0)
    t_lo, c_ge = search_half(c_ge)
    thr = ((t_hi_s << 16) | t_lo)

    jfin_ref[...] = jnp.full((1, QL), INT_MAX, I32)

    @pl.when(jnp.max(c_ge) > n_sel)
    def _():
        c_gt = count_keys(lambda kk, base: kk > thr)
        need = n_sel - c_gt

        def idx_bit(it, jlo):
            cand = jlo | lax.shift_left(jnp.int32(1), idx_bits - 1 - it)
            c = count_keys(lambda kk, base: (kk == thr) & ((k_iota + base) < cand))
            return jnp.where(c < need, cand, jlo)

        jlo = lax.fori_loop(0, idx_bits, idx_bit, jnp.zeros((1, QL), I32))
        jfin_ref[...] = jnp.where(c_ge > n_sel, jlo, INT_MAX)

    jfin = jfin_ref[...]
    m_ref[...] = jnp.full(m_ref.shape, -jnp.inf, F32)
    acc_ref[...] = jnp.zeros(acc_ref.shape, F32)
    ones = jnp.ones((tk, HEAD_DIM), BF16)
    oh_row = lax.broadcasted_iota(I32, (GROUP * tq, QL), 0) & (tq - 1)
    onehot = (oh_row == lax.broadcasted_iota(I32, (GROUP * tq, QL), 1)).astype(BF16)
    for g in range(N_KV_HEADS):
        qg = q_ref[g * GROUP:(g + 1) * GROUP].reshape(GROUP * tq, HEAD_DIM)
        qx_ref[g] = jnp.concatenate([qg, onehot], axis=1)

    def attend_tile(j, kt, vt):
        kk = key_ref[j]
        kpos = j * tk + k_iota
        sel = (kk > thr) | ((kk == thr) & (kpos <= jfin))
        sel = sel & (kpos < q_lim)
        bias = jnp.where(sel, 0.0, MASKED).astype(BF16)
        kt = kt.astype(BF16)
        vt = vt.astype(BF16)
        for g in range(N_KV_HEADS):
            kx = jnp.concatenate([kt[:, g * HEAD_DIM:(g + 1) * HEAD_DIM], bias], axis=1)
            s = _dot_nt(qx_ref[g], kx)
            m_prev = m_ref[g]
            m_new = jnp.maximum(m_prev, jnp.max(s, axis=-1, keepdims=True))
            alpha = jnp.exp2(m_prev - m_new)
            p = jnp.exp2(s - jnp.tile(m_new, (1, tk // LANES))).astype(BF16)
            vx = jnp.concatenate([vt[:, g * HEAD_DIM:(g + 1) * HEAD_DIM], ones], axis=1)
            acc_ref[g] = jnp.tile(alpha, (1, 2)) * acc_ref[g] + jnp.dot(p, vx, preferred_element_type=F32)
            m_ref[g] = m_new

    for_tiles(lambda j: attend_tile(j, rows(k_ref, j), rows(v_ref, j)))
    if n_cache_tiles is not None:
        attend_tile(n_cache_tiles, kn_ref[...], vn_ref[...])

    for g in range(N_KV_HEADS):
        acc = acc_ref[g]
        o = acc[:, :HEAD_DIM] / acc[:, HEAD_DIM:]
        for u in range(GROUP):
            h = g * GROUP + u
            z = z_ref[:, h * HEAD_DIM:(h + 1) * HEAD_DIM]
            gated = o[u * tq:(u + 1) * tq] * (z * _sigmoid(z))
            o_ref[:, h * HEAD_DIM:(h + 1) * HEAD_DIM] = gated.astype(BF16)


def _dsa_attention(q_hm, qi_hm, wi_hm, z, ki, k, v, new_keys=None, *, n_batch, n_qblk, tq, tk, per_batch_rows,
                   q_pos_base, l_valid):
    lk = ki.shape[1]
    assert tq & (tq - 1) == 0 and tq <= QL and lk % tk == 0
    n_tiles = lk // tk + (0 if new_keys is None else 1)
    n_sel = min(MAX_SELECT, l_valid // 4)
    idx_bits = max(1, (n_tiles * tk - 1).bit_length())
    if per_batch_rows:
        qmap = lambda b, i: (b, 0, i, 0)
        rmap = lambda b, i: (b, i, 0)
        imap = qmap
        wmap = lambda b, i: (b, 0, i)
    else:
        qmap = lambda b, i: (0, 0, b, 0)
        rmap = lambda b, i: (0, b, 0)
        imap = lambda b, i: (b, 0, 0, 0)
        wmap = lambda b, i: (b, 0, 0)
    kmap = lambda b, i: (b, 0, 0)
    kern = functools.partial(_dsa_kernel, tq=tq, tk=tk, q_pos_base=q_pos_base, l_valid=l_valid,
                             n_sel=n_sel, idx_bits=idx_bits,
                             n_cache_tiles=None if new_keys is None else lk // tk)
    key_specs = [
        pl.BlockSpec((None, lk, IDX_DIM), kmap),
        pl.BlockSpec((None, lk, KV_WIDTH), kmap),
        pl.BlockSpec((None, lk, KV_WIDTH), kmap),
    ]
    if new_keys is not None:
        key_specs += [
            pl.BlockSpec((None, tk, IDX_DIM), kmap),
            pl.BlockSpec((None, tk, KV_WIDTH), kmap),
            pl.BlockSpec((None, tk, KV_WIDTH), kmap),
        ]
    return pl.pallas_call(
        kern,
        out_shape=jax.ShapeDtypeStruct(z.shape, BF16),
        grid=(n_batch, n_qblk),
        in_specs=[
            pl.BlockSpec((None, N_HEADS, tq, HEAD_DIM), qmap),
            pl.BlockSpec((None, IDX_HEADS, QL, IDX_DIM), imap),
            pl.BlockSpec((None, IDX_HEADS, QL), wmap),
            pl.BlockSpec((None, tq, ATT_WIDTH), rmap),
        ] + key_specs,
        out_specs=pl.BlockSpec((None, tq, ATT_WIDTH), rmap),
        scratch_shapes=[
            pltpu.VMEM((n_tiles, tk, QL), I32),
            pltpu.VMEM((n_tiles, tk, QL), I16),
            pltpu.VMEM((1, QL), I32),
            pltpu.VMEM((N_KV_HEADS, GROUP * tq, LANES), F32),
            pltpu.VMEM((N_KV_HEADS, GROUP * tq, 2 * HEAD_DIM), F32),
            pltpu.VMEM((N_KV_HEADS, GROUP * tq, HEAD_DIM + QL), BF16),
        ],
        compiler_params=_params("parallel", "arbitrary"),
        name="dsa_attention",
    )(q_hm, qi_hm, wi_hm, z, ki, k, v, *(new_keys or ()))


def _rec_in_kernel(x_ref, nw_ref, w_ref, p_ref, *, col_chunk):
    xn = _rms_scale(x_ref[...], nw_ref[...]).astype(BF16)
    for c in range(0, w_ref.shape[1], col_chunk):
        p_ref[:, c:c + col_chunk] = jnp.dot(xn, w_ref[:, c:c + col_chunk], preferred_element_type=F32)


def _rec_in_proj(x, nw, w, tm):
    B, L, D = x.shape
    n_out = w.shape[1]
    return pl.pallas_call(
        functools.partial(_rec_in_kernel, col_chunk=1024),
        out_shape=jax.ShapeDtypeStruct((B, L, n_out), F32),
        grid=(B, L // tm),
        in_specs=[
            pl.BlockSpec((None, tm, D), lambda b, i: (b, i, 0)),
            pl.BlockSpec((1, D), lambda b, i: (0, 0)),
            pl.BlockSpec(w.shape, lambda b, i: (0, 0)),
        ],
        out_specs=pl.BlockSpec((None, tm, n_out), lambda b, i: (b, i, 0)),
        compiler_params=_params("parallel", "parallel"),
        name="rec_in_proj",
    )(x, nw.reshape(1, D), w)


def _rec_kernel(qp_ref, fp_ref, v_ref, z_ref, lbl_ref, gn_ref, s0_ref, o_ref, sout_ref,
                st_ref, k_all, b_all, v_all, *, layer, ch, n_ch):
    t = pl.program_id(2)

    @pl.when(t == 0)
    def _():
        for hh in range(REC_HPS):
            st_ref[hh] = s0_ref[hh].T

    logits = lbl_ref[...]
    e = jnp.exp(logits - jnp.max(logits, axis=0, keepdims=True))
    p_lb = e / jnp.sum(e, axis=0, keepdims=True)
    lb_all = jnp.zeros((1, REC_HPS * REC_DK), F32)
    for r in range(1, layer + 1):
        lb_all = lb_all + p_lb[r:r + 1]
    gn_all = gn_ref[...]

    tri = (lax.broadcasted_iota(I32, (ch, ch), 0) >= lax.broadcasted_iota(I32, (ch, ch), 1)).astype(F32)
    half_row = lax.broadcasted_iota(I32, (SUB // 2, REC_DK), 0)
    ch_row = lax.broadcasted_iota(I32, (ch, REC_DK), 0)

    def chunk(c, st, slot, hh):
        r = pl.multiple_of(c * ch, ch)
        lanes = slice(hh * REC_DK, (hh + 1) * REC_DK)
        lb = lb_all[:, lanes]
        gn = gn_all[:, lanes]
        qp = qp_ref[pl.ds(r, ch), lanes]
        f = lb + (1.0 - lb) * _sigmoid(fp_ref[pl.ds(r, ch), lanes])
        q = qp * _sigmoid(qp)
        g = jnp.log(f)
        kk = 1.0 - f
        v = v_ref[pl.ds(r, ch), lanes]
        b = jnp.dot(tri, g, preferred_element_type=F32, precision=lax.Precision.HIGHEST)
        k_s, b_s, v_s = k_all.at[slot], b_all.at[slot], v_all.at[slot]
        k_s[...] = kk
        b_s[...] = b
        v_s[...] = v
        vb = v.astype(BF16)

        o_inter = _dot_nt((q * jnp.exp(b)).astype(BF16), st.astype(BF16))

        for blk in range(ch // SUB):
            r0 = blk * SUB
            q_i = q[r0:r0 + SUB]
            b_i = b[r0:r0 + SUB]
            o_i = o_inter[r0:r0 + SUB]
            half = SUB // 2
            o_lo, o_hi = o_i[:half], o_i[half:]
            for s in range(SUB):
                b_row = b_s[r0 + s:r0 + s + 1, :]
                k_row = k_s[r0 + s:r0 + s + 1, :]
                v_row = v_s[r0 + s:r0 + s + 1, :]
                if s < half:
                    a = q_i[:half] * (k_row * jnp.exp(b_i[:half] - b_row))
                    a = jnp.where(half_row >= s, a, 0.0)
                    o_lo = o_lo + jnp.sum(a, axis=-1, keepdims=True) * v_row
                a = q_i[half:] * (k_row * jnp.exp(b_i[half:] - b_row))
                if s > half:
                    a = jnp.where(half_row >= s - half, a, 0.0)
                o_hi = o_hi + jnp.sum(a, axis=-1, keepdims=True) * v_row
            o_i = jnp.concatenate([o_lo, o_hi], axis=0)
            if blk > 0:
                c_i = b[r0 - 1:r0]
                a = (q_i * jnp.exp(b_i - c_i)).astype(BF16)
                kd = jnp.where(ch_row < r0, kk * jnp.exp(jnp.minimum(c_i - b, 0.0)), 0.0).astype(BF16)
                pm = _dot_nt(a, kd)
                o_i = o_i + jnp.dot(pm.astype(BF16), vb, preferred_element_type=F32)
            o_n = o_i * lax.rsqrt(jnp.mean(o_i * o_i, axis=-1, keepdims=True) + EPS) * gn
            z = z_ref[pl.ds(r + r0, SUB), lanes]
            o_ref[pl.ds(r + r0, SUB), lanes] = (o_n * (z * _sigmoid(z))).astype(o_ref.dtype)

        b_last = b[ch - 1:ch]
        kd = (kk * jnp.exp(b_last - b)).astype(BF16)
        return st * jnp.exp(b_last) + _dot_tn(vb, kd)

    heads = range(REC_HPS)
    sts = tuple(st_ref[hh] for hh in heads)
    if n_ch % 2 == 0:
        def pair(c2, sts):
            return tuple(chunk(2 * c2 + 1, chunk(2 * c2, sts[hh], 2 * hh, hh), 2 * hh + 1, hh) for hh in heads)
        sts = lax.fori_loop(0, n_ch // 2, pair, sts)
    else:
        sts = lax.fori_loop(0, n_ch, lambda c, sts: tuple(chunk(c, sts[hh], 2 * hh, hh) for hh in heads), sts)
    for hh in heads:
        st_ref[hh] = sts[hh]

    @pl.when(t == pl.num_programs(2) - 1)
    def _():
        for hh in heads:
            sout_ref[hh] = st_ref[hh].T


def _rec_scan(p, lb_logits, gnorm, s0, *, layer, tl, ch):
    B, L, _ = p.shape
    H = REC_HEADS
    hw = REC_HPS * REC_DK
    col = lambda off: pl.BlockSpec((None, tl, hw), lambda b, h, t: (b, t, off // REC_HPS + h))
    n_rec = lb_logits.shape[0]
    st_spec = pl.BlockSpec((None, REC_HPS, REC_DK, REC_DV), lambda b, h, t: (b, h, 0, 0))
    return pl.pallas_call(
        functools.partial(_rec_kernel, layer=layer, ch=ch, n_ch=tl // ch),
        out_shape=(jax.ShapeDtypeStruct((B, L, H * REC_DV), BF16),
                   jax.ShapeDtypeStruct(s0.shape, F32)),
        grid=(B, H // REC_HPS, L // tl),
        in_specs=[
            col(0), col(H), col(2 * H), col(3 * H),
            pl.BlockSpec((n_rec, hw), lambda b, h, t: (0, h)),
            pl.BlockSpec((1, hw), lambda b, h, t: (0, h)),
            st_spec,
        ],
        out_specs=(pl.BlockSpec((None, tl, hw), lambda b, h, t: (b, t, h)), st_spec),
        scratch_shapes=[pltpu.VMEM((REC_HPS, REC_DV, REC_DK), F32)]
        + [pltpu.VMEM((2 * REC_HPS, ch, REC_DK), F32)] * 3,
        compiler_params=_params("parallel", "parallel", "arbitrary"),
        name="rec_scan",
    )(p, p, p, p, lb_logits, gnorm.reshape(1, H * REC_DV), s0)


def _out_kernel(a_ref, w_ref, h_ref, fw_ref, o_ref, *, final_norm):
    h = h_ref[...] + jnp.dot(a_ref[...].astype(BF16), w_ref[...], preferred_element_type=F32)
    if final_norm:
        h = _rms_scale(h, fw_ref[...])
    o_ref[...] = h


def _out_proj(a, w, h, fw, tm, final_norm):
    B, L, D = h.shape
    tok = lambda width: pl.BlockSpec((None, tm, width), lambda b, i: (b, i, 0))
    return pl.pallas_call(
        functools.partial(_out_kernel, final_norm=final_norm),
        out_shape=jax.ShapeDtypeStruct(h.shape, F32),
        grid=(B, L // tm),
        in_specs=[tok(a.shape[-1]), pl.BlockSpec(w.shape, lambda b, i: (0, 0)), tok(D),
                  pl.BlockSpec((1, D), lambda b, i: (0, 0))],
        out_specs=tok(D),
        compiler_params=_params("parallel", "parallel"),
        name="out_proj",
    )(a, w, h, fw.reshape(1, D))


def _rope_tables(pos):
    half = HEAD_DIM // 2
    inv = ROPE_THETA ** (-jnp.arange(half, dtype=F32) / half)
    ang = pos.astype(F32)[:, None] * inv[None, :]
    cos = jnp.cos(ang)
    sin = jnp.sin(ang)
    return jnp.concatenate([cos, cos], axis=-1), jnp.concatenate([-sin, sin], axis=-1)


def _row_tile(n, pref):
    return pref if n % pref == 0 else n


def kernel(x_prompt, x_sample, cache_k, cache_v, cache_kidx, state_rec, norm_w, final_norm_w,
           att_w_in, att_w_out, rec_w_in, rec_w_out, rec_gnorm_w, rec_lb_logits):
    B, L, D = x_prompt.shape
    DB, T, _ = x_sample.shape
    past = cache_k.shape[2]
    depth = norm_w.shape[0]
    n_mix = 2

    n_main = att_w_in.shape[-1] - IDX_HEADS
    w_att = att_w_in[:, :, :n_main].astype(BF16)
    w_wi_t = jnp.pad(jnp.swapaxes(att_w_in[:, :, n_main:], 1, 2), ((0, 0), (0, 16 - IDX_HEADS), (0, 0))).astype(BF16)
    w_att_out = att_w_out.astype(BF16)
    w_rec = rec_w_in.astype(BF16)
    w_rec_out = rec_w_out.astype(BF16)

    cos_p, sin_p = _rope_tables(jnp.arange(L, dtype=I32))
    cos_s, sin_s = _rope_tables(past + jnp.arange(T, dtype=I32))
    cos_s = jnp.tile(cos_s, (DB, 1))
    sin_s = jnp.tile(sin_s, (DB, 1))

    tk = 512
    assert past % tk == 0 and T <= tk

    h = x_prompt
    tm = _row_tile(L, 256)
    tq = _row_tile(L, 128)
    tl = _row_tile(L, 512)
    n_att = att_w_in.shape[0]
    new_stacks = lambda nb, nl: tuple(jnp.zeros((n_att, nb, nl, w), F32) for w in (KV_WIDTH, KV_WIDTH, IDX_DIM))
    st_p = new_stacks(B, L)
    sp = []
    for i in range(depth):
        j = i // n_mix
        last = i == depth - 1
        if i % n_mix == 0:
            q, qi, wi, z, *st_p, kb, vb, kib = _att_in_proj(h, norm_w[i], w_att[j], w_wi_t[j], cos_p, sin_p, tm,
                                                            layer=j, stacks=st_p)
            a = _dsa_attention(q, qi, wi, z, kib, kb, vb, n_batch=B, n_qblk=L // tq, tq=tq,
                               tk=min(tk, L), per_batch_rows=True, q_pos_base=0, l_valid=L)
            h = _out_proj(a, w_att_out[j], h, final_norm_w, tm, last)
        else:
            p = _rec_in_proj(h, norm_w[i], w_rec[j], tm)
            s0 = jnp.zeros((B, REC_HEADS, REC_DK, REC_DV), F32)
            a, s = _rec_scan(p, rec_lb_logits, rec_gnorm_w[j], s0, layer=j, tl=tl, ch=min(CHUNK, L))
            h = _out_proj(a, w_rec_out[j], h, final_norm_w, tm, last)
            sp.append(s)
    y_prompt = h

    n_rows = DB * T
    h = x_sample.reshape(1, n_rows, D)
    st_s = new_stacks(1, n_rows)
    ss_ = []
    for i in range(depth):
        j = i // n_mix
        last = i == depth - 1
        if i % n_mix == 0:
            q, qi, wi, z, *st_s, _, _, _ = _att_in_proj(h, norm_w[i], w_att[j], w_wi_t[j], cos_s, sin_s, n_rows,
                                                        layer=j, stacks=st_s)
            k, v, ki = (a[j] for a in st_s)
            qi = jnp.pad(jnp.swapaxes(qi.reshape(IDX_HEADS, DB, T, IDX_DIM), 0, 1),
                         ((0, 0), (0, 0), (0, QL - T), (0, 0)))
            wi = jnp.pad(jnp.swapaxes(wi.reshape(IDX_HEADS, DB, T), 0, 1), ((0, 0), (0, 0), (0, QL - T)))
            new_tile = lambda a: jnp.pad(a.reshape(DB, T, -1), ((0, 0), (0, tk - T), (0, 0)))
            a = _dsa_attention(q, qi, wi, z, cache_kidx[j], cache_k[j].reshape(DB, past, KV_WIDTH),
                               cache_v[j].reshape(DB, past, KV_WIDTH), (new_tile(ki), new_tile(k), new_tile(v)),
                               n_batch=DB, n_qblk=1, tq=T, tk=tk, per_batch_rows=False,
                               q_pos_base=past, l_valid=past + T)
            h = _out_proj(a, w_att_out[j], h, final_norm_w, n_rows, last)
        else:
            p = _rec_in_proj(h, norm_w[i], w_rec[j], n_rows)
            a, s = _rec_scan(p.reshape(DB, T, -1), rec_lb_logits, rec_gnorm_w[j], state_rec[j],
                             layer=j, tl=T, ch=T)
            h = _out_proj(a.reshape(1, n_rows, -1), w_rec_out[j], h, final_norm_w, n_rows, last)
            ss_.append(s)
    y_sample = h.reshape(DB, T, D)

    return (y_prompt, y_sample,
            st_p[0].reshape(n_att, B, L, N_KV_HEADS, HEAD_DIM), st_p[1].reshape(n_att, B, L, N_KV_HEADS, HEAD_DIM),
            st_p[2], jnp.stack(sp),
            st_s[0].reshape(n_att, DB, T, N_KV_HEADS, HEAD_DIM), st_s[1].reshape(n_att, DB, T, N_KV_HEADS, HEAD_DIM),
            st_s[2].reshape(n_att, DB, T, IDX_DIM), jnp.stack(ss_))
```

```python
import functools

import jax
import jax.numpy as jnp
from jax import lax
from jax.experimental import pallas as pl
from jax.experimental.pallas import tpu as pltpu

F32 = jnp.float32
BF16 = jnp.bfloat16
I32 = jnp.int32

EPS = 1e-6
CHUNK = 64
LANES = 128

N_HEADS = 8
HEAD_DIM = 128
N_KV_HEADS = 2
GROUP = N_HEADS // N_KV_HEADS
IDX_HEADS = 8
IDX_DIM = 128
MAX_SELECT = 256
ROPE_THETA = 10000.0
ATT_SCALE = HEAD_DIM ** -0.5
IDX_SCALE = (IDX_HEADS * IDX_DIM) ** -0.5
ATT_WIDTH = N_HEADS * HEAD_DIM
KV_WIDTH = N_KV_HEADS * HEAD_DIM
Q_PRESCALE = ATT_SCALE * 1.4426950408889634
QL = LANES
SCORE_GROUP = 4
ATTEND_GROUP = 8

REC_HEADS = 8
REC_DK = 128
REC_DV = 128
SUB = 16
REC_HPS = 4

VMEM_LIMIT = 56 * 1024 * 1024
MASKED = -1e30
INT_MIN = -2 ** 31
INT_MAX = 2 ** 31 - 1


def _params(*sem):
    return pltpu.CompilerParams(dimension_semantics=sem, vmem_limit_bytes=VMEM_LIMIT)


def _sigmoid(x):
    return 1.0 / (1.0 + jnp.exp(-x))


def _rms_scale(x, w):
    ms = jnp.mean(x * x, axis=-1, keepdims=True)
    return x * lax.rsqrt(ms + EPS) * w


def _dot_nt(a, b):
    return lax.dot_general(a, b, (((1,), (1,)), ((), ())), preferred_element_type=F32)


def _dot_tn(a, b):
    return lax.dot_general(a, b, (((0,), (0,)), ((), ())), preferred_element_type=F32)


def _att_in_kernel(*refs):
    x_ref, nw_ref, w_ref, wt_ref, cos_ref, sin_ref = refs[:6]
    q_ref, qi_ref, wi_ref, z_ref, k_ref, v_ref, ki_ref, kb_ref, vb_ref, kib_ref = refs[-10:]
    xn = _rms_scale(x_ref[...], nw_ref[...]).astype(BF16)
    cos = cos_ref[...]
    sin = sin_ref[...]

    def rope(t):
        return t * cos + pltpu.roll(t, HEAD_DIM // 2, 1) * sin

    def proj(c0, c1):
        return jnp.dot(xn, w_ref[:, c0:c1], preferred_element_type=F32)

    c = 0
    pq = proj(c, c + ATT_WIDTH)
    for h in range(N_HEADS):
        q_ref[h] = (rope(pq[:, h * HEAD_DIM:(h + 1) * HEAD_DIM]) * Q_PRESCALE).astype(BF16)
    c += ATT_WIDTH
    pkv = proj(c, c + 2 * KV_WIDTH)
    for g in range(N_KV_HEADS):
        kg = rope(pkv[:, g * HEAD_DIM:(g + 1) * HEAD_DIM])
        k_ref[:, g * HEAD_DIM:(g + 1) * HEAD_DIM] = kg
        kb_ref[:, g * HEAD_DIM:(g + 1) * HEAD_DIM] = kg.astype(BF16)
    vv = pkv[:, KV_WIDTH:]
    v_ref[...] = vv
    vb_ref[...] = vv.astype(BF16)
    c += 2 * KV_WIDTH
    z_ref[...] = proj(c, c + ATT_WIDTH)
    c += ATT_WIDTH
    pqi = proj(c, c + IDX_HEADS * IDX_DIM)
    for h in range(IDX_HEADS):
        qi_ref[h] = rope(pqi[:, h * IDX_DIM:(h + 1) * IDX_DIM]).astype(BF16)
    c += IDX_HEADS * IDX_DIM
    ki = rope(proj(c, c + IDX_DIM))
    ki_ref[...] = ki
    kib_ref[...] = ki.astype(BF16)
    wi_ref[...] = _dot_nt(wt_ref[...], xn)[:IDX_HEADS] * IDX_SCALE


def _att_in_proj(x, nw, w_main, w_wi_t, cos, sin, tm, *, layer, stacks):
    n_layers = stacks[0].shape[0]
    B, L, D = x.shape
    grid = (B, L // tm)
    tok = lambda width: pl.BlockSpec((None, tm, width), lambda b, i: (b, i, 0))
    hm = pl.BlockSpec((None, N_HEADS, tm, HEAD_DIM), lambda b, i: (b, 0, i, 0))
    slab = lambda width: pl.BlockSpec((None, None, tm, width), lambda b, i: (layer, b, i, 0))
    stacked = lambda width: jax.ShapeDtypeStruct((n_layers, B, L, width), F32)
    out_shape = (
        jax.ShapeDtypeStruct((B, N_HEADS, L, HEAD_DIM), BF16),
        jax.ShapeDtypeStruct((B, IDX_HEADS, L, IDX_DIM), BF16),
        jax.ShapeDtypeStruct((B, IDX_HEADS, L), F32),
        jax.ShapeDtypeStruct((B, L, ATT_WIDTH), F32),
        stacked(KV_WIDTH),
        stacked(KV_WIDTH),
        stacked(IDX_DIM),
        jax.ShapeDtypeStruct((B, L, KV_WIDTH), BF16),
        jax.ShapeDtypeStruct((B, L, KV_WIDTH), BF16),
        jax.ShapeDtypeStruct((B, L, IDX_DIM), BF16),
    )
    wi_spec = pl.BlockSpec((None, IDX_HEADS, tm), lambda b, i: (b, 0, i))
    out_specs = (hm, hm, wi_spec, tok(ATT_WIDTH), slab(KV_WIDTH), slab(KV_WIDTH), slab(IDX_DIM),
                 tok(KV_WIDTH), tok(KV_WIDTH), tok(IDX_DIM))
    in_specs = [
        tok(D),
        pl.BlockSpec((1, D), lambda b, i: (0, 0)),
        pl.BlockSpec(w_main.shape, lambda b, i: (0, 0)),
        pl.BlockSpec(w_wi_t.shape, lambda b, i: (0, 0)),
        pl.BlockSpec((tm, HEAD_DIM), lambda b, i: (i, 0)),
        pl.BlockSpec((tm, HEAD_DIM), lambda b, i: (i, 0)),
    ]
    args = [x, nw.reshape(1, D), w_main, w_wi_t, cos, sin]
    in_specs += [pl.BlockSpec(memory_space=pl.ANY)] * 3
    aliases = {len(args) + n: 4 + n for n in range(3)}
    args += list(stacks)
    return pl.pallas_call(
        _att_in_kernel,
        out_shape=out_shape,
        grid=grid,
        in_specs=in_specs,
        out_specs=out_specs,
        input_output_aliases=aliases,
        compiler_params=_params("parallel", "parallel"),
        name="att_in_proj",
    )(*args)


def _dsa_kernel(*refs, tq, tk, q_pos_base, l_valid, n_sel, idx_bits, n_cache_tiles):
    if n_cache_tiles is None:
        q_ref, qi_ref, wi_ref, z_ref, ki_ref, k_ref, v_ref, o_ref = refs[:8]
    else:
        q_ref, qi_ref, wi_ref, z_ref, ki_ref, k_ref, v_ref, kin_ref, kn_ref, vn_ref, o_ref = refs[:11]
    key_ref, jfin_ref, m_ref, acc_ref, qx_ref = refs[-5:]
    i = pl.program_id(1)
    q0 = q_pos_base + i * tq
    q_pos = q0 + lax.broadcasted_iota(I32, (1, QL), 1)
    q_lim = jnp.minimum((q_pos // CHUNK + 1) * CHUNK, l_valid)
    last_lim = jnp.minimum(((q0 + tq - 1) // CHUNK + 1) * CHUNK, l_valid)
    nk = (last_lim + tk - 1) // tk
    k_iota = lax.broadcasted_iota(I32, (tk, 1), 0)

    qi = qi_ref[...].reshape(IDX_HEADS * QL, IDX_DIM)
    wi = wi_ref[...]

    nk_main = nk if n_cache_tiles is None else n_cache_tiles

    def for_tiles(body, group):
        start = 0
        while group >= 1:
            def grouped(p, carry, group=group, start=start):
                for u in range(group):
                    body(start + group * p + u)
                return carry
            n_groups = (nk_main - start) // group
            lax.fori_loop(0, n_groups, grouped, 0)
            start = start + n_groups * group
            group //= 2

    def rows(ref, j):
        return ref[pl.ds(pl.multiple_of(j * tk, tk), tk), :]

    def score_tile(j, kt):
        s = _dot_nt(kt.astype(BF16), qi)
        acc = jnp.zeros((tk, QL), F32)
        for h in range(IDX_HEADS):
            acc = acc + wi[h:h + 1, :] * jnp.maximum(s[:, h * QL:(h + 1) * QL], 0.0)
        adm = (j * tk + k_iota) < q_lim
        sc = jnp.where(adm, acc, -jnp.inf)
        sc = jnp.where(sc == 0.0, 0.0, sc)
        bits = lax.bitcast_convert_type(sc, I32)
        key = bits ^ ((bits >> 31) & INT_MAX)
        key_ref[j] = key

    for_tiles(lambda j: score_tile(j, rows(ki_ref, j)), SCORE_GROUP)
    if n_cache_tiles is not None:
        score_tile(n_cache_tiles, kin_ref[...])

    def count_keys(pred):
        def body(j, cnt):
            hit = pred(key_ref[j], j * tk).astype(I32)
            return cnt + jnp.sum(hit.reshape(tk // 8, 8, QL), axis=0)
        cnt = lax.fori_loop(0, nk, body, jnp.zeros((8, QL), I32))
        return jnp.sum(cnt, axis=0, keepdims=True)

    def search_bit(it, carry):
        t_biased, c_ge = carry
        cand = t_biased | lax.shift_left(jnp.int32(1), 31 - it)
        cand_s = cand ^ INT_MIN
        c = count_keys(lambda kk, base: kk >= cand_s)
        ok = c >= n_sel
        return jnp.where(ok, cand, t_biased), jnp.where(ok, c, c_ge)

    t_biased, c_ge = lax.fori_loop(
        0, 32, search_bit, (jnp.zeros((1, QL), I32), jnp.zeros((1, QL), I32) + nk * tk))
    thr = t_biased ^ INT_MIN

    jfin_ref[...] = jnp.full((1, QL), INT_MAX, I32)

    @pl.when(jnp.max(c_ge) > n_sel)
    def _():
        c_gt = count_keys(lambda kk, base: kk > thr)
        need = n_sel - c_gt

        def idx_bit(it, jlo):
            cand = jlo | lax.shift_left(jnp.int32(1), idx_bits - 1 - it)
            c = count_keys(lambda kk, base: (kk == thr) & ((k_iota + base) < cand))
            return jnp.where(c < need, cand, jlo)

        jlo = lax.fori_loop(0, idx_bits, idx_bit, jnp.zeros((1, QL), I32))
        jfin_ref[...] = jnp.where(c_ge > n_sel, jlo, INT_MAX)

    jfin = jfin_ref[...]
    m_ref[...] = jnp.full(m_ref.shape, -jnp.inf, F32)
    acc_ref[...] = jnp.zeros(acc_ref.shape, F32)
    ones = jnp.ones((tk, HEAD_DIM), BF16)
    oh_row = lax.broadcasted_iota(I32, (GROUP * tq, QL), 0) & (tq - 1)
    onehot = (oh_row == lax.broadcasted_iota(I32, (GROUP * tq, QL), 1)).astype(BF16)
    for g in range(N_KV_HEADS):
        qg = q_ref[g * GROUP:(g + 1) * GROUP].reshape(GROUP * tq, HEAD_DIM)
        qx_ref[g] = jnp.concatenate([qg, onehot], axis=1)

    def attend_tile(j, kt, vt):
        kk = key_ref[j]
        kpos = j * tk + k_iota
        sel = (kk > thr) | ((kk == thr) & (kpos <= jfin))
        sel = sel & (kpos < q_lim)
        bias = jnp.where(sel, 0.0, MASKED).astype(BF16)
        kt = kt.astype(BF16)
        vt = vt.astype(BF16)
        for g in range(N_KV_HEADS):
            kx = jnp.concatenate([kt[:, g * HEAD_DIM:(g + 1) * HEAD_DIM], bias], axis=1)
            s = _dot_nt(qx_ref[g], kx)
            m_prev = m_ref[g]
            m_new = jnp.maximum(m_prev, jnp.max(s, axis=-1, keepdims=True))
            alpha = jnp.exp2(m_prev - m_new)
            p = jnp.exp2(s - jnp.tile(m_new, (1, tk // LANES))).astype(BF16)
            vx = jnp.concatenate([vt[:, g * HEAD_DIM:(g + 1) * HEAD_DIM], ones], axis=1)
            acc_ref[g] = jnp.tile(alpha, (1, 2)) * acc_ref[g] + jnp.dot(p, vx, preferred_element_type=F32)
            m_ref[g] = m_new

    for_tiles(lambda j: attend_tile(j, rows(k_ref, j), rows(v_ref, j)), ATTEND_GROUP)
    if n_cache_tiles is not None:
        attend_tile(n_cache_tiles, kn_ref[...], vn_ref[...])

    for g in range(N_KV_HEADS):
        acc = acc_ref[g]
        o = acc[:, :HEAD_DIM] / acc[:, HEAD_DIM:]
        for u in range(GROUP):
            h = g * GROUP + u
            z = z_ref[:, h * HEAD_DIM:(h + 1) * HEAD_DIM]
            gated = o[u * tq:(u + 1) * tq] * (z * _sigmoid(z))
            o_ref[:, h * HEAD_DIM:(h + 1) * HEAD_DIM] = gated.astype(BF16)


def _dsa_attention(q_hm, qi_hm, wi_hm, z, ki, k, v, new_keys=None, *, n_batch, n_qblk, tq, tk, per_batch_rows,
                   q_pos_base, l_valid):
    lk = ki.shape[1]
    assert tq & (tq - 1) == 0 and tq <= QL and lk % tk == 0
    n_tiles = lk // tk + (0 if new_keys is None else 1)
    n_sel = min(MAX_SELECT, l_valid // 4)
    idx_bits = max(1, (n_tiles * tk - 1).bit_length())
    if per_batch_rows:
        qmap = lambda b, i: (b, 0, i, 0)
        rmap = lambda b, i: (b, i, 0)
        imap = qmap
        wmap = lambda b, i: (b, 0, i)
    else:
        qmap = lambda b, i: (0, 0, b, 0)
        rmap = lambda b, i: (0, b, 0)
        imap = lambda b, i: (b, 0, 0, 0)
        wmap = lambda b, i: (b, 0, 0)
    kmap = lambda b, i: (b, 0, 0)
    kern = functools.partial(_dsa_kernel, tq=tq, tk=tk, q_pos_base=q_pos_base, l_valid=l_valid,
                             n_sel=n_sel, idx_bits=idx_bits,
                             n_cache_tiles=None if new_keys is None else lk // tk)
    key_specs = [
        pl.BlockSpec((None, lk, IDX_DIM), kmap),
        pl.BlockSpec((None, lk, KV_WIDTH), kmap),
        pl.BlockSpec((None, lk, KV_WIDTH), kmap),
    ]
    if new_keys is not None:
        key_specs += [
            pl.BlockSpec((None, tk, IDX_DIM), kmap),
            pl.BlockSpec((None, tk, KV_WIDTH), kmap),
            pl.BlockSpec((None, tk, KV_WIDTH), kmap),
        ]
    return pl.pallas_call(
        kern,
        out_shape=jax.ShapeDtypeStruct(z.shape, BF16),
        grid=(n_batch, n_qblk),
        in_specs=[
            pl.BlockSpec((None, N_HEADS, tq, HEAD_DIM), qmap),
            pl.BlockSpec((None, IDX_HEADS, QL, IDX_DIM), imap),
            pl.BlockSpec((None, IDX_HEADS, QL), wmap),
            pl.BlockSpec((None, tq, ATT_WIDTH), rmap),
        ] + key_specs,
        out_specs=pl.BlockSpec((None, tq, ATT_WIDTH), rmap),
        scratch_shapes=[
            pltpu.VMEM((n_tiles, tk, QL), I32),
            pltpu.VMEM((1, QL), I32),
            pltpu.VMEM((N_KV_HEADS, GROUP * tq, LANES), F32),
            pltpu.VMEM((N_KV_HEADS, GROUP * tq, 2 * HEAD_DIM), F32),
            pltpu.VMEM((N_KV_HEADS, GROUP * tq, HEAD_DIM + QL), BF16),
        ],
        compiler_params=_params("parallel", "arbitrary"),
        name="dsa_attention",
    )(q_hm, qi_hm, wi_hm, z, ki, k, v, *(new_keys or ()))


def _rec_in_kernel(x_ref, nw_ref, w_ref, p_ref, *, col_chunk):
    xn = _rms_scale(x_ref[...], nw_ref[...]).astype(BF16)
    for c in range(0, w_ref.shape[1], col_chunk):
        p_ref[:, c:c + col_chunk] = jnp.dot(xn, w_ref[:, c:c + col_chunk], preferred_element_type=F32)


def _rec_in_proj(x, nw, w, tm):
    B, L, D = x.shape
    n_out = w.shape[1]
    return pl.pallas_call(
        functools.partial(_rec_in_kernel, col_chunk=1024),
        out_shape=jax.ShapeDtypeStruct((B, L, n_out), F32),
        grid=(B, L // tm),
        in_specs=[
            pl.BlockSpec((None, tm, D), lambda b, i: (b, i, 0)),
            pl.BlockSpec((1, D), lambda b, i: (0, 0)),
            pl.BlockSpec(w.shape, lambda b, i: (0, 0)),
        ],
        out_specs=pl.BlockSpec((None, tm, n_out), lambda b, i: (b, i, 0)),
        compiler_params=_params("parallel", "parallel"),
        name="rec_in_proj",
    )(x, nw.reshape(1, D), w)


def _rec_kernel(qp_ref, fp_ref, v_ref, z_ref, lbl_ref, gn_ref, s0_ref, o_ref, sout_ref,
                st_ref, k_all, b_all, v_all, *, layer, ch, n_ch):
    t = pl.program_id(2)

    @pl.when(t == 0)
    def _():
        for hh in range(REC_HPS):
            st_ref[hh] = s0_ref[hh].T

    logits = lbl_ref[...]
    e = jnp.exp(logits - jnp.max(logits, axis=0, keepdims=True))
    p_lb = e / jnp.sum(e, axis=0, keepdims=True)
    lb_all = jnp.zeros((1, REC_HPS * REC_DK), F32)
    for r in range(1, layer + 1):
        lb_all = lb_all + p_lb[r:r + 1]
    gn_all = gn_ref[...]

    tri = (lax.broadcasted_iota(I32, (ch, ch), 0) >= lax.broadcasted_iota(I32, (ch, ch), 1)).astype(F32)
    half_row = lax.broadcasted_iota(I32, (SUB // 2, REC_DK), 0)
    ch_row = lax.broadcasted_iota(I32, (ch, REC_DK), 0)

    def chunk(c, st, slot, hh):
        r = pl.multiple_of(c * ch, ch)
        lanes = slice(hh * REC_DK, (hh + 1) * REC_DK)
        lb = lb_all[:, lanes]
        gn = gn_all[:, lanes]
        qp = qp_ref[pl.ds(r, ch), lanes]
        f = lb + (1.0 - lb) * _sigmoid(fp_ref[pl.ds(r, ch), lanes])
        q = qp * _sigmoid(qp)
        g = jnp.log(f)
        kk = 1.0 - f
        v = v_ref[pl.ds(r, ch), lanes]
        b = jnp.dot(tri, g, preferred_element_type=F32, precision=lax.Precision.HIGHEST)
        k_s, b_s, v_s = k_all.at[slot], b_all.at[slot], v_all.at[slot]
        k_s[...] = kk
        b_s[...] = b
        v_s[...] = v
        vb = v.astype(BF16)

        o_inter = _dot_nt((q * jnp.exp(b)).astype(BF16), st.astype(BF16))

        for blk in range(ch // SUB):
            r0 = blk * SUB
            q_i = q[r0:r0 + SUB]
            b_i = b[r0:r0 + SUB]
            o_i = o_inter[r0:r0 + SUB]
            half = SUB // 2
            o_lo, o_hi = o_i[:half], o_i[half:]
            for s in range(SUB):
                b_row = b_s[r0 + s:r0 + s + 1, :]
                k_row = k_s[r0 + s:r0 + s + 1, :]
                v_row = v_s[r0 + s:r0 + s + 1, :]
                if s < half:
                    a = q_i[:half] * (k_row * jnp.exp(b_i[:half] - b_row))
                    a = jnp.where(half_row >= s, a, 0.0)
                    o_lo = o_lo + jnp.sum(a, axis=-1, keepdims=True) * v_row
                a = q_i[half:] * (k_row * jnp.exp(b_i[half:] - b_row))
                if s > half:
                    a = jnp.where(half_row >= s - half, a, 0.0)
                o_hi = o_hi + jnp.sum(a, axis=-1, keepdims=True) * v_row
            o_i = jnp.concatenate([o_lo, o_hi], axis=0)
            if blk > 0:
                c_i = b[r0 - 1:r0]
                a = (q_i * jnp.exp(b_i - c_i)).astype(BF16)
                kd = jnp.where(ch_row < r0, kk * jnp.exp(jnp.minimum(c_i - b, 0.0)), 0.0).astype(BF16)
                pm = _dot_nt(a, kd)
                o_i = o_i + jnp.dot(pm.astype(BF16), vb, preferred_element_type=F32)
            o_n = o_i * lax.rsqrt(jnp.mean(o_i * o_i, axis=-1, keepdims=True) + EPS) * gn
            z = z_ref[pl.ds(r + r0, SUB), lanes]
            o_ref[pl.ds(r + r0, SUB), lanes] = (o_n * (z * _sigmoid(z))).astype(o_ref.dtype)

        b_last = b[ch - 1:ch]
        kd = (kk * jnp.exp(b_last - b)).astype(BF16)
        return st * jnp.exp(b_last) + _dot_tn(vb, kd)

    heads = range(REC_HPS)
    sts = tuple(st_ref[hh] for hh in heads)
    if n_ch % 2 == 0:
        def pair(c2, sts):
            return tuple(chunk(2 * c2 + 1, chunk(2 * c2, sts[hh], 2 * hh, hh), 2 * hh + 1, hh) for hh in heads)
        sts = lax.fori_loop(0, n_ch // 2, pair, sts)
    else:
        sts = lax.fori_loop(0, n_ch, lambda c, sts: tuple(chunk(c, sts[hh], 2 * hh, hh) for hh in heads), sts)
    for hh in heads:
        st_ref[hh] = sts[hh]

    @pl.when(t == pl.num_programs(2) - 1)
    def _():
        for hh in heads:
            sout_ref[hh] = st_ref[hh].T


def _rec_scan(p, lb_logits, gnorm, s0, *, layer, tl, ch):
    B, L, _ = p.shape
    H = REC_HEADS
    hw = REC_HPS * REC_DK
    col = lambda off: pl.BlockSpec((None, tl, hw), lambda b, h, t: (b, t, off // REC_HPS + h))
    n_rec = lb_logits.shape[0]
    st_spec = pl.BlockSpec((None, REC_HPS, REC_DK, REC_DV), lambda b, h, t: (b, h, 0, 0))
    return pl.pallas_call(
        functools.partial(_rec_kernel, layer=layer, ch=ch, n_ch=tl // ch),
        out_shape=(jax.ShapeDtypeStruct((B, L, H * REC_DV), BF16),
                   jax.ShapeDtypeStruct(s0.shape, F32)),
        grid=(B, H // REC_HPS, L // tl),
        in_specs=[
            col(0), col(H), col(2 * H), col(3 * H),
            pl.BlockSpec((n_rec, hw), lambda b, h, t: (0, h)),
            pl.BlockSpec((1, hw), lambda b, h, t: (0, h)),
            st_spec,
        ],
        out_specs=(pl.BlockSpec((None, tl, hw), lambda b, h, t: (b, t, h)), st_spec),
        scratch_shapes=[pltpu.VMEM((REC_HPS, REC_DV, REC_DK), F32)]
        + [pltpu.VMEM((2 * REC_HPS, ch, REC_DK), F32)] * 3,
        compiler_params=_params("parallel", "parallel", "arbitrary"),
        name="rec_scan",
    )(p, p, p, p, lb_logits, gnorm.reshape(1, H * REC_DV), s0)


def _out_kernel(a_ref, w_ref, h_ref, fw_ref, o_ref, *, final_norm):
    h = h_ref[...] + jnp.dot(a_ref[...].astype(BF16), w_ref[...], preferred_element_type=F32)
    if final_norm:
        h = _rms_scale(h, fw_ref[...])
    o_ref[...] = h


def _out_proj(a, w, h, fw, tm, final_norm):
    B, L, D = h.shape
    tok = lambda width: pl.BlockSpec((None, tm, width), lambda b, i: (b, i, 0))
    return pl.pallas_call(
        functools.partial(_out_kernel, final_norm=final_norm),
        out_shape=jax.ShapeDtypeStruct(h.shape, F32),
        grid=(B, L // tm),
        in_specs=[tok(a.shape[-1]), pl.BlockSpec(w.shape, lambda b, i: (0, 0)), tok(D),
                  pl.BlockSpec((1, D), lambda b, i: (0, 0))],
        out_specs=tok(D),
        compiler_params=_params("parallel", "parallel"),
        name="out_proj",
    )(a, w, h, fw.reshape(1, D))


def _rope_tables(pos):
    half = HEAD_DIM // 2
    inv = ROPE_THETA ** (-jnp.arange(half, dtype=F32) / half)
    ang = pos.astype(F32)[:, None] * inv[None, :]
    cos = jnp.cos(ang)
    sin = jnp.sin(ang)
    return jnp.concatenate([cos, cos], axis=-1), jnp.concatenate([-sin, sin], axis=-1)


def _row_tile(n, pref):
    return pref if n % pref == 0 else n


def kernel(x_prompt, x_sample, cache_k, cache_v, cache_kidx, state_rec, norm_w, final_norm_w,
           att_w_in, att_w_out, rec_w_in, rec_w_out, rec_gnorm_w, rec_lb_logits):
    B, L, D = x_prompt.shape
    DB, T, _ = x_sample.shape
    past = cache_k.shape[2]
    depth = norm_w.shape[0]
    n_mix = 2

    n_main = att_w_in.shape[-1] - IDX_HEADS
    w_att = att_w_in[:, :, :n_main].astype(BF16)
    w_wi_t = jnp.pad(jnp.swapaxes(att_w_in[:, :, n_main:], 1, 2), ((0, 0), (0, 16 - IDX_HEADS), (0, 0))).astype(BF16)
    w_att_out = att_w_out.astype(BF16)
    w_rec = rec_w_in.astype(BF16)
    w_rec_out = rec_w_out.astype(BF16)

    cos_p, sin_p = _rope_tables(jnp.arange(L, dtype=I32))
    cos_s, sin_s = _rope_tables(past + jnp.arange(T, dtype=I32))
    cos_s = jnp.tile(cos_s, (DB, 1))
    sin_s = jnp.tile(sin_s, (DB, 1))

    tk = 512
    assert past % tk == 0 and T <= tk

    h = x_prompt
    tm = _row_tile(L, 256)
    tq = _row_tile(L, 128)
    tl = _row_tile(L, 512)
    n_att = att_w_in.shape[0]
    new_stacks = lambda nb, nl: tuple(jnp.zeros((n_att, nb, nl, w), F32) for w in (KV_WIDTH, KV_WIDTH, IDX_DIM))
    st_p = new_stacks(B, L)
    sp = []
    for i in range(depth):
        j = i // n_mix
        last = i == depth - 1
        if i % n_mix == 0:
            q, qi, wi, z, *st_p, kb, vb, kib = _att_in_proj(h, norm_w[i], w_att[j], w_wi_t[j], cos_p, sin_p, tm,
                                                            layer=j, stacks=st_p)
            a = _dsa_attention(q, qi, wi, z, kib, kb, vb, n_batch=B, n_qblk=L // tq, tq=tq,
                               tk=min(tk, L), per_batch_rows=True, q_pos_base=0, l_valid=L)
            h = _out_proj(a, w_att_out[j], h, final_norm_w, tm, last)
        else:
            p = _rec_in_proj(h, norm_w[i], w_rec[j], tm)
            s0 = jnp.zeros((B, REC_HEADS, REC_DK, REC_DV), F32)
            a, s = _rec_scan(p, rec_lb_logits, rec_gnorm_w[j], s0, layer=j, tl=tl, ch=min(CHUNK, L))
            h = _out_proj(a, w_rec_out[j], h, final_norm_w, tm, last)
            sp.append(s)
    y_prompt = h

    n_rows = DB * T
    h = x_sample.reshape(1, n_rows, D)
    st_s = new_stacks(1, n_rows)
    ss_ = []
    for i in range(depth):
        j = i // n_mix
        last = i == depth - 1
        if i % n_mix == 0:
            q, qi, wi, z, *st_s, _, _, _ = _att_in_proj(h, norm_w[i], w_att[j], w_wi_t[j], cos_s, sin_s, n_rows,
                                                        layer=j, stacks=st_s)
            k, v, ki = (a[j] for a in st_s)
            qi = jnp.pad(jnp.swapaxes(qi.reshape(IDX_HEADS, DB, T, IDX_DIM), 0, 1),
                         ((0, 0), (0, 0), (0, QL - T), (0, 0)))
            wi = jnp.pad(jnp.swapaxes(wi.reshape(IDX_HEADS, DB, T), 0, 1), ((0, 0), (0, 0), (0, QL - T)))
            new_tile = lambda a: jnp.pad(a.reshape(DB, T, -1), ((0, 0), (0, tk - T), (0, 0)))
            a = _dsa_attention(q, qi, wi, z, cache_kidx[j], cache_k[j].reshape(DB, past, KV_WIDTH),
                               cache_v[j].reshape(DB, past, KV_WIDTH), (new_tile(ki), new_tile(k), new_tile(v)),
                               n_batch=DB, n_qblk=1, tq=T, tk=tk, per_batch_rows=False,
                               q_pos_base=past, l_valid=past + T)
            h = _out_proj(a, w_att_out[j], h, final_norm_w, n_rows, last)
        else:
            p = _rec_in_proj(h, norm_w[i], w_rec[j], n_rows)
            a, s = _rec_scan(p.reshape(DB, T, -1), rec_lb_logits, rec_gnorm_w[j], state_rec[j],
                             layer=j, tl=T, ch=T)
            h = _out_proj(a.reshape(1, n_rows, -1), w_rec_out[j], h, final_norm_w, n_rows, last)
            ss_.append(s)
    y_sample = h.reshape(DB, T, D)

    return (y_prompt, y_sample,
            st_p[0].reshape(n_att, B, L, N_KV_HEADS, HEAD_DIM), st_p[1].reshape(n_att, B, L, N_KV_HEADS, HEAD_DIM),
            st_p[2], jnp.stack(sp),
            st_s[0].reshape(n_att, DB, T, N_KV_HEADS, HEAD_DIM), st_s[1].reshape(n_att, DB, T, N_KV_HEADS, HEAD_DIM),
            st_s[2].reshape(n_att, DB, T, IDX_DIM), jnp.stack(ss_))
```

```python
import functools

import jax
import jax.numpy as jnp
from jax import lax
from jax.experimental import pallas as pl
from jax.experimental.pallas import tpu as pltpu

F32 = jnp.float32
BF16 = jnp.bfloat16
I32 = jnp.int32

EPS = 1e-6
CHUNK = 64
LANES = 128

N_HEADS = 8
HEAD_DIM = 128
N_KV_HEADS = 2
GROUP = N_HEADS // N_KV_HEADS
IDX_HEADS = 8
IDX_DIM = 128
MAX_SELECT = 256
ROPE_THETA = 10000.0
ATT_SCALE = HEAD_DIM ** -0.5
IDX_SCALE = (IDX_HEADS * IDX_DIM) ** -0.5
ATT_WIDTH = N_HEADS * HEAD_DIM
KV_WIDTH = N_KV_HEADS * HEAD_DIM
Q_PRESCALE = ATT_SCALE * 1.4426950408889634
QL = LANES
SCORE_GROUP = 4
ATTEND_GROUP = 8

REC_HEADS = 8
REC_DK = 128
REC_DV = 128
SUB = 16
REC_HPS = 4

VMEM_LIMIT = 56 * 1024 * 1024
MASKED = -1e30
INT_MIN = -2 ** 31
INT_MAX = 2 ** 31 - 1


def _params(*sem):
    return pltpu.CompilerParams(dimension_semantics=sem, vmem_limit_bytes=VMEM_LIMIT)


def _sigmoid(x):
    return 1.0 / (1.0 + jnp.exp(-x))


def _rms_scale(x, w):
    ms = jnp.mean(x * x, axis=-1, keepdims=True)
    return x * lax.rsqrt(ms + EPS) * w


def _dot_nt(a, b):
    return lax.dot_general(a, b, (((1,), (1,)), ((), ())), preferred_element_type=F32)


def _dot_tn(a, b):
    return lax.dot_general(a, b, (((0,), (0,)), ((), ())), preferred_element_type=F32)


def _att_in_kernel(*refs):
    x_ref, nw_ref, w_ref, wt_ref, cos_ref, sin_ref = refs[:6]
    q_ref, qi_ref, wi_ref, z_ref, k_ref, v_ref, ki_ref, kb_ref, vb_ref, kib_ref = refs[-10:]
    xn = _rms_scale(x_ref[...], nw_ref[...]).astype(BF16)
    cos = cos_ref[...]
    sin = sin_ref[...]

    def rope(t):
        return t * cos + pltpu.roll(t, HEAD_DIM // 2, 1) * sin

    def proj(c0, c1):
        return jnp.dot(xn, w_ref[:, c0:c1], preferred_element_type=F32)

    c = 0
    pq = proj(c, c + ATT_WIDTH)
    for h in range(N_HEADS):
        q_ref[h] = (rope(pq[:, h * HEAD_DIM:(h + 1) * HEAD_DIM]) * Q_PRESCALE).astype(BF16)
    c += ATT_WIDTH
    pkv = proj(c, c + 2 * KV_WIDTH)
    for g in range(N_KV_HEADS):
        kg = rope(pkv[:, g * HEAD_DIM:(g + 1) * HEAD_DIM])
        k_ref[:, g * HEAD_DIM:(g + 1) * HEAD_DIM] = kg
        kb_ref[:, g * HEAD_DIM:(g + 1) * HEAD_DIM] = kg.astype(BF16)
    vv = pkv[:, KV_WIDTH:]
    v_ref[...] = vv
    vb_ref[...] = vv.astype(BF16)
    c += 2 * KV_WIDTH
    z_ref[...] = proj(c, c + ATT_WIDTH)
    c += ATT_WIDTH
    pqi = proj(c, c + IDX_HEADS * IDX_DIM)
    for h in range(IDX_HEADS):
        qi_ref[h] = rope(pqi[:, h * IDX_DIM:(h + 1) * IDX_DIM]).astype(BF16)
    c += IDX_HEADS * IDX_DIM
    ki = rope(proj(c, c + IDX_DIM))
    ki_ref[...] = ki
    kib_ref[...] = ki.astype(BF16)
    wi_ref[...] = _dot_nt(wt_ref[...], xn)[:IDX_HEADS] * IDX_SCALE


def _att_in_proj(x, nw, w_main, w_wi_t, cos, sin, tm, *, layer, stacks):
    n_layers = stacks[0].shape[0]
    B, L, D = x.shape
    grid = (B, L // tm)
    tok = lambda width: pl.BlockSpec((None, tm, width), lambda b, i: (b, i, 0))
    hm = pl.BlockSpec((None, N_HEADS, tm, HEAD_DIM), lambda b, i: (b, 0, i, 0))
    slab = lambda width: pl.BlockSpec((None, None, tm, width), lambda b, i: (layer, b, i, 0))
    stacked = lambda width: jax.ShapeDtypeStruct((n_layers, B, L, width), F32)
    out_shape = (
        jax.ShapeDtypeStruct((B, N_HEADS, L, HEAD_DIM), BF16),
        jax.ShapeDtypeStruct((B, IDX_HEADS, L, IDX_DIM), BF16),
        jax.ShapeDtypeStruct((B, IDX_HEADS, L), F32),
        jax.ShapeDtypeStruct((B, L, ATT_WIDTH), F32),
        stacked(KV_WIDTH),
        stacked(KV_WIDTH),
        stacked(IDX_DIM),
        jax.ShapeDtypeStruct((B, L, KV_WIDTH), BF16),
        jax.ShapeDtypeStruct((B, L, KV_WIDTH), BF16),
        jax.ShapeDtypeStruct((B, L, IDX_DIM), BF16),
    )
    wi_spec = pl.BlockSpec((None, IDX_HEADS, tm), lambda b, i: (b, 0, i))
    out_specs = (hm, hm, wi_spec, tok(ATT_WIDTH), slab(KV_WIDTH), slab(KV_WIDTH), slab(IDX_DIM),
                 tok(KV_WIDTH), tok(KV_WIDTH), tok(IDX_DIM))
    in_specs = [
        tok(D),
        pl.BlockSpec((1, D), lambda b, i: (0, 0)),
        pl.BlockSpec(w_main.shape, lambda b, i: (0, 0)),
        pl.BlockSpec(w_wi_t.shape, lambda b, i: (0, 0)),
        pl.BlockSpec((tm, HEAD_DIM), lambda b, i: (i, 0)),
        pl.BlockSpec((tm, HEAD_DIM), lambda b, i: (i, 0)),
    ]
    args = [x, nw.reshape(1, D), w_main, w_wi_t, cos, sin]
    in_specs += [pl.BlockSpec(memory_space=pl.ANY)] * 3
    aliases = {len(args) + n: 4 + n for n in range(3)}
    args += list(stacks)
    return pl.pallas_call(
        _att_in_kernel,
        out_shape=out_shape,
        grid=grid,
        in_specs=in_specs,
        out_specs=out_specs,
        input_output_aliases=aliases,
        compiler_params=_params("parallel", "parallel"),
        name="att_in_proj",
    )(*args)


def _dsa_kernel(*refs, tq, tk, q_pos_base, l_valid, n_sel, idx_bits, n_cache_tiles):
    if n_cache_tiles is None:
        q_ref, qi_ref, wi_ref, z_ref, ki_ref, k_ref, v_ref, o_ref = refs[:8]
    else:
        q_ref, qi_ref, wi_ref, z_ref, ki_ref, k_ref, v_ref, kin_ref, kn_ref, vn_ref, o_ref = refs[:11]
    key_ref, jfin_ref, m_ref, acc_ref, qx_ref = refs[-5:]
    i = pl.program_id(1)
    q0 = q_pos_base + i * tq
    q_pos = q0 + lax.broadcasted_iota(I32, (1, QL), 1)
    q_lim = jnp.minimum((q_pos // CHUNK + 1) * CHUNK, l_valid)
    last_lim = jnp.minimum(((q0 + tq - 1) // CHUNK + 1) * CHUNK, l_valid)
    nk = (last_lim + tk - 1) // tk
    k_iota = lax.broadcasted_iota(I32, (tk, 1), 0)

    qi = qi_ref[...].reshape(IDX_HEADS * QL, IDX_DIM)
    wi = wi_ref[...]

    nk_main = nk if n_cache_tiles is None else n_cache_tiles

    def for_tiles(body, group):
        start = 0
        while group >= 1:
            def grouped(p, carry, group=group, start=start):
                for u in range(group):
                    body(start + group * p + u)
                return carry
            n_groups = (nk_main - start) // group
            lax.fori_loop(0, n_groups, grouped, 0)
            start = start + n_groups * group
            group //= 2

    def rows(ref, j):
        return ref[pl.ds(pl.multiple_of(j * tk, tk), tk), :]

    def score_tile(j, kt):
        s = _dot_nt(kt.astype(BF16), qi)
        acc = jnp.zeros((tk, QL), F32)
        for h in range(IDX_HEADS):
            acc = acc + wi[h:h + 1, :] * jnp.maximum(s[:, h * QL:(h + 1) * QL], 0.0)
        adm = (j * tk + k_iota) < q_lim
        sc = jnp.where(adm, acc, -jnp.inf)
        sc = jnp.where(sc == 0.0, 0.0, sc)
        bits = lax.bitcast_convert_type(sc, I32)
        key = bits ^ ((bits >> 31) & INT_MAX)
        key_ref[j] = key

    for_tiles(lambda j: score_tile(j, rows(ki_ref, j)), SCORE_GROUP)
    if n_cache_tiles is not None:
        score_tile(n_cache_tiles, kin_ref[...])

    def count_keys(pred):
        def body(j, cnt):
            hit = pred(key_ref[j], j * tk).astype(I32)
            return cnt + jnp.sum(hit.reshape(tk // 8, 8, QL), axis=0)
        cnt = lax.fori_loop(0, nk, body, jnp.zeros((8, QL), I32))
        return jnp.sum(cnt, axis=0, keepdims=True)

    n_grp = 256
    assert n_sel <= n_grp and tk % n_grp == 0

    def group_max(j, gm):
        kk = key_ref[j]
        for c in range(tk // n_grp):
            gm = jnp.maximum(gm, kk[c * n_grp:(c + 1) * n_grp])
        return gm

    gm = lax.fori_loop(0, nk, group_max, jnp.full((n_grp, QL), INT_MIN, I32))
    lo = jnp.min(gm, axis=0, keepdims=True)
    hi = jnp.max(gm, axis=0, keepdims=True)
    c_ge = count_keys(lambda kk, base: kk >= lo)

    width = hi - lo
    n_steps = jnp.zeros((1, QL), I32)
    for b in range(32):
        n_steps = n_steps + ((width ^ INT_MIN) >= ((1 << b) - 2 ** 31)).astype(I32)
    n_steps = jnp.max(n_steps)

    def bisect(it, carry):
        lo, hi, c_ge = carry
        width = hi - lo
        mid = lo + lax.shift_right_logical(width, 1) + (width & 1)
        c = count_keys(lambda kk, base: kk >= mid)
        ok = c >= n_sel
        return jnp.where(ok, mid, lo), jnp.where(ok, hi, mid - 1), jnp.where(ok, c, c_ge)

    thr, _, c_ge = lax.fori_loop(0, n_steps, bisect, (lo, hi, c_ge))

    jfin_ref[...] = jnp.full((1, QL), INT_MAX, I32)

    @pl.when(jnp.max(c_ge) > n_sel)
    def _():
        c_gt = count_keys(lambda kk, base: kk > thr)
        need = n_sel - c_gt

        def idx_bit(it, jlo):
            cand = jlo | lax.shift_left(jnp.int32(1), idx_bits - 1 - it)
            c = count_keys(lambda kk, base: (kk == thr) & ((k_iota + base) < cand))
            return jnp.where(c < need, cand, jlo)

        jlo = lax.fori_loop(0, idx_bits, idx_bit, jnp.zeros((1, QL), I32))
        jfin_ref[...] = jnp.where(c_ge > n_sel, jlo, INT_MAX)

    jfin = jfin_ref[...]
    m_ref[...] = jnp.full(m_ref.shape, -jnp.inf, F32)
    acc_ref[...] = jnp.zeros(acc_ref.shape, F32)
    ones = jnp.ones((tk, HEAD_DIM), BF16)
    oh_row = lax.broadcasted_iota(I32, (GROUP * tq, QL), 0) & (tq - 1)
    onehot = (oh_row == lax.broadcasted_iota(I32, (GROUP * tq, QL), 1)).astype(BF16)
    for g in range(N_KV_HEADS):
        qg = q_ref[g * GROUP:(g + 1) * GROUP].reshape(GROUP * tq, HEAD_DIM)
        qx_ref[g] = jnp.concatenate([qg, onehot], axis=1)

    def attend_tile(j, kt, vt):
        kk = key_ref[j]
        kpos = j * tk + k_iota
        sel = (kk > thr) | ((kk == thr) & (kpos <= jfin))
        sel = sel & (kpos < q_lim)
        bias = jnp.where(sel, 0.0, MASKED).astype(BF16)
        kt = kt.astype(BF16)
        vt = vt.astype(BF16)
        for g in range(N_KV_HEADS):
            kx = jnp.concatenate([kt[:, g * HEAD_DIM:(g + 1) * HEAD_DIM], bias], axis=1)
            s = _dot_nt(qx_ref[g], kx)
            m_prev = m_ref[g]
            m_new = jnp.maximum(m_prev, jnp.max(s, axis=-1, keepdims=True))
            alpha = jnp.exp2(m_prev - m_new)
            p = jnp.exp2(s - jnp.tile(m_new, (1, tk // LANES))).astype(BF16)
            vx = jnp.concatenate([vt[:, g * HEAD_DIM:(g + 1) * HEAD_DIM], ones], axis=1)
            acc_ref[g] = jnp.tile(alpha, (1, 2)) * acc_ref[g] + jnp.dot(p, vx, preferred_element_type=F32)
            m_ref[g] = m_new

    for_tiles(lambda j: attend_tile(j, rows(k_ref, j), rows(v_ref, j)), ATTEND_GROUP)
    if n_cache_tiles is not None:
        attend_tile(n_cache_tiles, kn_ref[...], vn_ref[...])

    for g in range(N_KV_HEADS):
        acc = acc_ref[g]
        o = acc[:, :HEAD_DIM] / acc[:, HEAD_DIM:]
        for u in range(GROUP):
            h = g * GROUP + u
            z = z_ref[:, h * HEAD_DIM:(h + 1) * HEAD_DIM]
            gated = o[u * tq:(u + 1) * tq] * (z * _sigmoid(z))
            o_ref[:, h * HEAD_DIM:(h + 1) * HEAD_DIM] = gated.astype(BF16)


def _dsa_attention(q_hm, qi_hm, wi_hm, z, ki, k, v, new_keys=None, *, n_batch, n_qblk, tq, tk, per_batch_rows,
                   q_pos_base, l_valid):
    lk = ki.shape[1]
    assert tq & (tq - 1) == 0 and tq <= QL and lk % tk == 0
    n_tiles = lk // tk + (0 if new_keys is None else 1)
    n_sel = min(MAX_SELECT, l_valid // 4)
    idx_bits = max(1, (n_tiles * tk - 1).bit_length())
    if per_batch_rows:
        qmap = lambda b, i: (b, 0, i, 0)
        rmap = lambda b, i: (b, i, 0)
        imap = qmap
        wmap = lambda b, i: (b, 0, i)
    else:
        qmap = lambda b, i: (0, 0, b, 0)
        rmap = lambda b, i: (0, b, 0)
        imap = lambda b, i: (b, 0, 0, 0)
        wmap = lambda b, i: (b, 0, 0)
    kmap = lambda b, i: (b, 0, 0)
    kern = functools.partial(_dsa_kernel, tq=tq, tk=tk, q_pos_base=q_pos_base, l_valid=l_valid,
                             n_sel=n_sel, idx_bits=idx_bits,
                             n_cache_tiles=None if new_keys is None else lk // tk)
    key_specs = [
        pl.BlockSpec((None, lk, IDX_DIM), kmap),
        pl.BlockSpec((None, lk, KV_WIDTH), kmap),
        pl.BlockSpec((None, lk, KV_WIDTH), kmap),
    ]
    if new_keys is not None:
        key_specs += [
            pl.BlockSpec((None, tk, IDX_DIM), kmap),
            pl.BlockSpec((None, tk, KV_WIDTH), kmap),
            pl.BlockSpec((None, tk, KV_WIDTH), kmap),
        ]
    return pl.pallas_call(
        kern,
        out_shape=jax.ShapeDtypeStruct(z.shape, BF16),
        grid=(n_batch, n_qblk),
        in_specs=[
            pl.BlockSpec((None, N_HEADS, tq, HEAD_DIM), qmap),
            pl.BlockSpec((None, IDX_HEADS, QL, IDX_DIM), imap),
            pl.BlockSpec((None, IDX_HEADS, QL), wmap),
            pl.BlockSpec((None, tq, ATT_WIDTH), rmap),
        ] + key_specs,
        out_specs=pl.BlockSpec((None, tq, ATT_WIDTH), rmap),
        scratch_shapes=[
            pltpu.VMEM((n_tiles, tk, QL), I32),
            pltpu.VMEM((1, QL), I32),
            pltpu.VMEM((N_KV_HEADS, GROUP * tq, LANES), F32),
            pltpu.VMEM((N_KV_HEADS, GROUP * tq, 2 * HEAD_DIM), F32),
            pltpu.VMEM((N_KV_HEADS, GROUP * tq, HEAD_DIM + QL), BF16),
        ],
        compiler_params=_params("parallel", "arbitrary"),
        name="dsa_attention",
    )(q_hm, qi_hm, wi_hm, z, ki, k, v, *(new_keys or ()))


def _rec_in_kernel(x_ref, nw_ref, w_ref, p_ref, *, col_chunk):
    xn = _rms_scale(x_ref[...], nw_ref[...]).astype(BF16)
    for c in range(0, w_ref.shape[1], col_chunk):
        p_ref[:, c:c + col_chunk] = jnp.dot(xn, w_ref[:, c:c + col_chunk], preferred_element_type=F32)


def _rec_in_proj(x, nw, w, tm):
    B, L, D = x.shape
    n_out = w.shape[1]
    return pl.pallas_call(
        functools.partial(_rec_in_kernel, col_chunk=1024),
        out_shape=jax.ShapeDtypeStruct((B, L, n_out), F32),
        grid=(B, L // tm),
        in_specs=[
            pl.BlockSpec((None, tm, D), lambda b, i: (b, i, 0)),
            pl.BlockSpec((1, D), lambda b, i: (0, 0)),
            pl.BlockSpec(w.shape, lambda b, i: (0, 0)),
        ],
        out_specs=pl.BlockSpec((None, tm, n_out), lambda b, i: (b, i, 0)),
        compiler_params=_params("parallel", "parallel"),
        name="rec_in_proj",
    )(x, nw.reshape(1, D), w)


def _rec_kernel(qp_ref, fp_ref, v_ref, z_ref, lbl_ref, gn_ref, s0_ref, o_ref, sout_ref,
                st_ref, k_all, b_all, v_all, *, layer, ch, n_ch):
    t = pl.program_id(2)

    @pl.when(t == 0)
    def _():
        for hh in range(REC_HPS):
            st_ref[hh] = s0_ref[hh].T

    logits = lbl_ref[...]
    e = jnp.exp(logits - jnp.max(logits, axis=0, keepdims=True))
    p_lb = e / jnp.sum(e, axis=0, keepdims=True)
    lb_all = jnp.zeros((1, REC_HPS * REC_DK), F32)
    for r in range(1, layer + 1):
        lb_all = lb_all + p_lb[r:r + 1]
    gn_all = gn_ref[...]

    tri = (lax.broadcasted_iota(I32, (ch, ch), 0) >= lax.broadcasted_iota(I32, (ch, ch), 1)).astype(F32)
    half_row = lax.broadcasted_iota(I32, (SUB // 2, REC_DK), 0)
    ch_row = lax.broadcasted_iota(I32, (ch, REC_DK), 0)

    def chunk(c, st, slot, hh):
        r = pl.multiple_of(c * ch, ch)
        lanes = slice(hh * REC_DK, (hh + 1) * REC_DK)
        lb = lb_all[:, lanes]
        gn = gn_all[:, lanes]
        qp = qp_ref[pl.ds(r, ch), lanes]
        f = lb + (1.0 - lb) * _sigmoid(fp_ref[pl.ds(r, ch), lanes])
        q = qp * _sigmoid(qp)
        g = jnp.log(f)
        kk = 1.0 - f
        v = v_ref[pl.ds(r, ch), lanes]
        b = jnp.dot(tri, g, preferred_element_type=F32, precision=lax.Precision.HIGHEST)
        k_s, b_s, v_s = k_all.at[slot], b_all.at[slot], v_all.at[slot]
        k_s[...] = kk
        b_s[...] = b
        v_s[...] = v
        vb = v.astype(BF16)

        o_inter = _dot_nt((q * jnp.exp(b)).astype(BF16), st.astype(BF16))

        for blk in range(ch // SUB):
            r0 = blk * SUB
            q_i = q[r0:r0 + SUB]
            b_i = b[r0:r0 + SUB]
            o_i = o_inter[r0:r0 + SUB]
            half = SUB // 2
            o_lo, o_hi = o_i[:half], o_i[half:]
            for s in range(SUB):
                b_row = b_s[r0 + s:r0 + s + 1, :]
                k_row = k_s[r0 + s:r0 + s + 1, :]
                v_row = v_s[r0 + s:r0 + s + 1, :]
                if s < half:
                    a = q_i[:half] * (k_row * jnp.exp(b_i[:half] - b_row))
                    a = jnp.where(half_row >= s, a, 0.0)
                    o_lo = o_lo + jnp.sum(a, axis=-1, keepdims=True) * v_row
                a = q_i[half:] * (k_row * jnp.exp(b_i[half:] - b_row))
                if s > half:
                    a = jnp.where(half_row >= s - half, a, 0.0)
                o_hi = o_hi + jnp.sum(a, axis=-1, keepdims=True) * v_row
            o_i = jnp.concatenate([o_lo, o_hi], axis=0)
            if blk > 0:
                c_i = b[r0 - 1:r0]
                a = (q_i * jnp.exp(b_i - c_i)).astype(BF16)
                kd = jnp.where(ch_row < r0, kk * jnp.exp(jnp.minimum(c_i - b, 0.0)), 0.0).astype(BF16)
                pm = _dot_nt(a, kd)
                o_i = o_i + jnp.dot(pm.astype(BF16), vb, preferred_element_type=F32)
            o_n = o_i * lax.rsqrt(jnp.mean(o_i * o_i, axis=-1, keepdims=True) + EPS) * gn
            z = z_ref[pl.ds(r + r0, SUB), lanes]
            o_ref[pl.ds(r + r0, SUB), lanes] = (o_n * (z * _sigmoid(z))).astype(o_ref.dtype)

        b_last = b[ch - 1:ch]
        kd = (kk * jnp.exp(b_last - b)).astype(BF16)
        return st * jnp.exp(b_last) + _dot_tn(vb, kd)

    heads = range(REC_HPS)
    sts = tuple(st_ref[hh] for hh in heads)
    if n_ch % 2 == 0:
        def pair(c2, sts):
            return tuple(chunk(2 * c2 + 1, chunk(2 * c2, sts[hh], 2 * hh, hh), 2 * hh + 1, hh) for hh in heads)
        sts = lax.fori_loop(0, n_ch // 2, pair, sts)
    else:
        sts = lax.fori_loop(0, n_ch, lambda c, sts: tuple(chunk(c, sts[hh], 2 * hh, hh) for hh in heads), sts)
    for hh in heads:
        st_ref[hh] = sts[hh]

    @pl.when(t == pl.num_programs(2) - 1)
    def _():
        for hh in heads:
            sout_ref[hh] = st_ref[hh].T


def _rec_scan(p, lb_logits, gnorm, s0, *, layer, tl, ch):
    B, L, _ = p.shape
    H = REC_HEADS
    hw = REC_HPS * REC_DK
    col = lambda off: pl.BlockSpec((None, tl, hw), lambda b, h, t: (b, t, off // REC_HPS + h))
    n_rec = lb_logits.shape[0]
    st_spec = pl.BlockSpec((None, REC_HPS, REC_DK, REC_DV), lambda b, h, t: (b, h, 0, 0))
    return pl.pallas_call(
        functools.partial(_rec_kernel, layer=layer, ch=ch, n_ch=tl // ch),
        out_shape=(jax.ShapeDtypeStruct((B, L, H * REC_DV), BF16),
                   jax.ShapeDtypeStruct(s0.shape, F32)),
        grid=(B, H // REC_HPS, L // tl),
        in_specs=[
            col(0), col(H), col(2 * H), col(3 * H),
            pl.BlockSpec((n_rec, hw), lambda b, h, t: (0, h)),
            pl.BlockSpec((1, hw), lambda b, h, t: (0, h)),
            st_spec,
        ],
        out_specs=(pl.BlockSpec((None, tl, hw), lambda b, h, t: (b, t, h)), st_spec),
        scratch_shapes=[pltpu.VMEM((REC_HPS, REC_DV, REC_DK), F32)]
        + [pltpu.VMEM((2 * REC_HPS, ch, REC_DK), F32)] * 3,
        compiler_params=_params("parallel", "parallel", "arbitrary"),
        name="rec_scan",
    )(p, p, p, p, lb_logits, gnorm.reshape(1, H * REC_DV), s0)


def _out_kernel(a_ref, w_ref, h_ref, fw_ref, o_ref, *, final_norm):
    h = h_ref[...] + jnp.dot(a_ref[...].astype(BF16), w_ref[...], preferred_element_type=F32)
    if final_norm:
        h = _rms_scale(h, fw_ref[...])
    o_ref[...] = h


def _out_proj(a, w, h, fw, tm, final_norm):
    B, L, D = h.shape
    tok = lambda width: pl.BlockSpec((None, tm, width), lambda b, i: (b, i, 0))
    return pl.pallas_call(
        functools.partial(_out_kernel, final_norm=final_norm),
        out_shape=jax.ShapeDtypeStruct(h.shape, F32),
        grid=(B, L // tm),
        in_specs=[tok(a.shape[-1]), pl.BlockSpec(w.shape, lambda b, i: (0, 0)), tok(D),
                  pl.BlockSpec((1, D), lambda b, i: (0, 0))],
        out_specs=tok(D),
        compiler_params=_params("parallel", "parallel"),
        name="out_proj",
    )(a, w, h, fw.reshape(1, D))


def _rope_tables(pos):
    half = HEAD_DIM // 2
    inv = ROPE_THETA ** (-jnp.arange(half, dtype=F32) / half)
    ang = pos.astype(F32)[:, None] * inv[None, :]
    cos = jnp.cos(ang)
    sin = jnp.sin(ang)
    return jnp.concatenate([cos, cos], axis=-1), jnp.concatenate([-sin, sin], axis=-1)


def _row_tile(n, pref):
    return pref if n % pref == 0 else n


def kernel(x_prompt, x_sample, cache_k, cache_v, cache_kidx, state_rec, norm_w, final_norm_w,
           att_w_in, att_w_out, rec_w_in, rec_w_out, rec_gnorm_w, rec_lb_logits):
    B, L, D = x_prompt.shape
    DB, T, _ = x_sample.shape
    past = cache_k.shape[2]
    depth = norm_w.shape[0]
    n_mix = 2

    n_main = att_w_in.shape[-1] - IDX_HEADS
    w_att = att_w_in[:, :, :n_main].astype(BF16)
    w_wi_t = jnp.pad(jnp.swapaxes(att_w_in[:, :, n_main:], 1, 2), ((0, 0), (0, 16 - IDX_HEADS), (0, 0))).astype(BF16)
    w_att_out = att_w_out.astype(BF16)
    w_rec = rec_w_in.astype(BF16)
    w_rec_out = rec_w_out.astype(BF16)

    cos_p, sin_p = _rope_tables(jnp.arange(L, dtype=I32))
    cos_s, sin_s = _rope_tables(past + jnp.arange(T, dtype=I32))
    cos_s = jnp.tile(cos_s, (DB, 1))
    sin_s = jnp.tile(sin_s, (DB, 1))

    tk = 512
    assert past % tk == 0 and T <= tk

    h = x_prompt
    tm = _row_tile(L, 256)
    tq = _row_tile(L, 128)
    tl = _row_tile(L, 512)
    n_att = att_w_in.shape[0]
    new_stacks = lambda nb, nl: tuple(jnp.zeros((n_att, nb, nl, w), F32) for w in (KV_WIDTH, KV_WIDTH, IDX_DIM))
    st_p = new_stacks(B, L)
    sp = []
    for i in range(depth):
        j = i // n_mix
        last = i == depth - 1
        if i % n_mix == 0:
            q, qi, wi, z, *st_p, kb, vb, kib = _att_in_proj(h, norm_w[i], w_att[j], w_wi_t[j], cos_p, sin_p, tm,
                                                            layer=j, stacks=st_p)
            a = _dsa_attention(q, qi, wi, z, kib, kb, vb, n_batch=B, n_qblk=L // tq, tq=tq,
                               tk=min(tk, L), per_batch_rows=True, q_pos_base=0, l_valid=L)
            h = _out_proj(a, w_att_out[j], h, final_norm_w, tm, last)
        else:
            p = _rec_in_proj(h, norm_w[i], w_rec[j], tm)
            s0 = jnp.zeros((B, REC_HEADS, REC_DK, REC_DV), F32)
            a, s = _rec_scan(p, rec_lb_logits, rec_gnorm_w[j], s0, layer=j, tl=tl, ch=min(CHUNK, L))
            h = _out_proj(a, w_rec_out[j], h, final_norm_w, tm, last)
            sp.append(s)
    y_prompt = h

    n_rows = DB * T
    h = x_sample.reshape(1, n_rows, D)
    st_s = new_stacks(1, n_rows)
    ss_ = []
    for i in range(depth):
        j = i // n_mix
        last = i == depth - 1
        if i % n_mix == 0:
            q, qi, wi, z, *st_s, _, _, _ = _att_in_proj(h, norm_w[i], w_att[j], w_wi_t[j], cos_s, sin_s, n_rows,
                                                        layer=j, stacks=st_s)
            k, v, ki = (a[j] for a in st_s)
            qi = jnp.pad(jnp.swapaxes(qi.reshape(IDX_HEADS, DB, T, IDX_DIM), 0, 1),
                         ((0, 0), (0, 0), (0, QL - T), (0, 0)))
            wi = jnp.pad(jnp.swapaxes(wi.reshape(IDX_HEADS, DB, T), 0, 1), ((0, 0), (0, 0), (0, QL - T)))
            new_tile = lambda a: jnp.pad(a.reshape(DB, T, -1), ((0, 0), (0, tk - T), (0, 0)))
            a = _dsa_attention(q, qi, wi, z, cache_kidx[j], cache_k[j].reshape(DB, past, KV_WIDTH),
                               cache_v[j].reshape(DB, past, KV_WIDTH), (new_tile(ki), new_tile(k), new_tile(v)),
                               n_batch=DB, n_qblk=1, tq=T, tk=tk, per_batch_rows=False,
                               q_pos_base=past, l_valid=past + T)
            h = _out_proj(a, w_att_out[j], h, final_norm_w, n_rows, last)
        else:
            p = _rec_in_proj(h, norm_w[i], w_rec[j], n_rows)
            a, s = _rec_scan(p.reshape(DB, T, -1), rec_lb_logits, rec_gnorm_w[j], state_rec[j],
                             layer=j, tl=T, ch=T)
            h = _out_proj(a.reshape(1, n_rows, -1), w_rec_out[j], h, final_norm_w, n_rows, last)
            ss_.append(s)
    y_sample = h.reshape(DB, T, D)

    return (y_prompt, y_sample,
            st_p[0].reshape(n_att, B, L, N_KV_HEADS, HEAD_DIM), st_p[1].reshape(n_att, B, L, N_KV_HEADS, HEAD_DIM),
            st_p[2], jnp.stack(sp),
            st_s[0].reshape(n_att, DB, T, N_KV_HEADS, HEAD_DIM), st_s[1].reshape(n_att, DB, T, N_KV_HEADS, HEAD_DIM),
            st_s[2].reshape(n_att, DB, T, IDX_DIM), jnp.stack(ss_))
```

```python
import functools

import jax
import jax.numpy as jnp
from jax import lax
from jax.experimental import pallas as pl
from jax.experimental.pallas import tpu as pltpu

F32 = jnp.float32
BF16 = jnp.bfloat16
I32 = jnp.int32

EPS = 1e-6
CHUNK = 64
LANES = 128

N_HEADS = 8
HEAD_DIM = 128
N_KV_HEADS = 2
GROUP = N_HEADS // N_KV_HEADS
IDX_HEADS = 8
IDX_DIM = 128
MAX_SELECT = 256
ROPE_THETA = 10000.0
ATT_SCALE = HEAD_DIM ** -0.5
IDX_SCALE = (IDX_HEADS * IDX_DIM) ** -0.5
ATT_WIDTH = N_HEADS * HEAD_DIM
KV_WIDTH = N_KV_HEADS * HEAD_DIM
Q_PRESCALE = ATT_SCALE * 1.4426950408889634
QL = LANES
SCORE_GROUP = 4
COUNT_GROUP = 4
ATTEND_GROUP = 8

REC_HEADS = 8
REC_DK = 128
REC_DV = 128
SUB = 16
REC_HPS = 4

VMEM_LIMIT = 56 * 1024 * 1024
MASKED = -1e30
INT_MIN = -2 ** 31
INT_MAX = 2 ** 31 - 1


def _params(*sem):
    return pltpu.CompilerParams(dimension_semantics=sem, vmem_limit_bytes=VMEM_LIMIT)


def _sigmoid(x):
    return 1.0 / (1.0 + jnp.exp(-x))


def _rms_scale(x, w):
    ms = jnp.mean(x * x, axis=-1, keepdims=True)
    return x * lax.rsqrt(ms + EPS) * w


def _dot_nt(a, b):
    return lax.dot_general(a, b, (((1,), (1,)), ((), ())), preferred_element_type=F32)


def _dot_tn(a, b):
    return lax.dot_general(a, b, (((0,), (0,)), ((), ())), preferred_element_type=F32)


def _att_in_kernel(*refs):
    x_ref, nw_ref, w_ref, wt_ref, cos_ref, sin_ref = refs[:6]
    q_ref, qi_ref, wi_ref, z_ref, k_ref, v_ref, ki_ref, kb_ref, vb_ref, kib_ref = refs[-10:]
    xn = _rms_scale(x_ref[...], nw_ref[...]).astype(BF16)
    cos = cos_ref[...]
    sin = sin_ref[...]

    def rope(t):
        return t * cos + pltpu.roll(t, HEAD_DIM // 2, 1) * sin

    def proj(c0, c1):
        return jnp.dot(xn, w_ref[:, c0:c1], preferred_element_type=F32)

    c = 0
    pq = proj(c, c + ATT_WIDTH)
    for h in range(N_HEADS):
        q_ref[h] = (rope(pq[:, h * HEAD_DIM:(h + 1) * HEAD_DIM]) * Q_PRESCALE).astype(BF16)
    c += ATT_WIDTH
    pkv = proj(c, c + 2 * KV_WIDTH)
    for g in range(N_KV_HEADS):
        kg = rope(pkv[:, g * HEAD_DIM:(g + 1) * HEAD_DIM])
        k_ref[:, g * HEAD_DIM:(g + 1) * HEAD_DIM] = kg
        kb_ref[:, g * HEAD_DIM:(g + 1) * HEAD_DIM] = kg.astype(BF16)
    vv = pkv[:, KV_WIDTH:]
    v_ref[...] = vv
    vb_ref[...] = vv.astype(BF16)
    c += 2 * KV_WIDTH
    z_ref[...] = proj(c, c + ATT_WIDTH)
    c += ATT_WIDTH
    pqi = proj(c, c + IDX_HEADS * IDX_DIM)
    for h in range(IDX_HEADS):
        qi_ref[h] = rope(pqi[:, h * IDX_DIM:(h + 1) * IDX_DIM]).astype(BF16)
    c += IDX_HEADS * IDX_DIM
    ki = rope(proj(c, c + IDX_DIM))
    ki_ref[...] = ki
    kib_ref[...] = ki.astype(BF16)
    wi_ref[...] = _dot_nt(wt_ref[...], xn)[:IDX_HEADS] * IDX_SCALE


def _att_in_proj(x, nw, w_main, w_wi_t, cos, sin, tm, *, layer, stacks):
    n_layers = stacks[0].shape[0]
    B, L, D = x.shape
    grid = (B, L // tm)
    tok = lambda width: pl.BlockSpec((None, tm, width), lambda b, i: (b, i, 0))
    hm = pl.BlockSpec((None, N_HEADS, tm, HEAD_DIM), lambda b, i: (b, 0, i, 0))
    slab = lambda width: pl.BlockSpec((None, None, tm, width), lambda b, i: (layer, b, i, 0))
    stacked = lambda width: jax.ShapeDtypeStruct((n_layers, B, L, width), F32)
    out_shape = (
        jax.ShapeDtypeStruct((B, N_HEADS, L, HEAD_DIM), BF16),
        jax.ShapeDtypeStruct((B, IDX_HEADS, L, IDX_DIM), BF16),
        jax.ShapeDtypeStruct((B, IDX_HEADS, L), F32),
        jax.ShapeDtypeStruct((B, L, ATT_WIDTH), F32),
        stacked(KV_WIDTH),
        stacked(KV_WIDTH),
        stacked(IDX_DIM),
        jax.ShapeDtypeStruct((B, L, KV_WIDTH), BF16),
        jax.ShapeDtypeStruct((B, L, KV_WIDTH), BF16),
        jax.ShapeDtypeStruct((B, L, IDX_DIM), BF16),
    )
    wi_spec = pl.BlockSpec((None, IDX_HEADS, tm), lambda b, i: (b, 0, i))
    out_specs = (hm, hm, wi_spec, tok(ATT_WIDTH), slab(KV_WIDTH), slab(KV_WIDTH), slab(IDX_DIM),
                 tok(KV_WIDTH), tok(KV_WIDTH), tok(IDX_DIM))
    in_specs = [
        tok(D),
        pl.BlockSpec((1, D), lambda b, i: (0, 0)),
        pl.BlockSpec(w_main.shape, lambda b, i: (0, 0)),
        pl.BlockSpec(w_wi_t.shape, lambda b, i: (0, 0)),
        pl.BlockSpec((tm, HEAD_DIM), lambda b, i: (i, 0)),
        pl.BlockSpec((tm, HEAD_DIM), lambda b, i: (i, 0)),
    ]
    args = [x, nw.reshape(1, D), w_main, w_wi_t, cos, sin]
    in_specs += [pl.BlockSpec(memory_space=pl.ANY)] * 3
    aliases = {len(args) + n: 4 + n for n in range(3)}
    args += list(stacks)
    return pl.pallas_call(
        _att_in_kernel,
        out_shape=out_shape,
        grid=grid,
        in_specs=in_specs,
        out_specs=out_specs,
        input_output_aliases=aliases,
        compiler_params=_params("parallel", "parallel"),
        name="att_in_proj",
    )(*args)


def _dsa_kernel(*refs, tq, tk, q_pos_base, l_valid, n_sel, idx_bits, n_cache_tiles):
    if n_cache_tiles is None:
        q_ref, qi_ref, wi_ref, z_ref, ki_ref, k_ref, v_ref, o_ref = refs[:8]
    else:
        q_ref, qi_ref, wi_ref, z_ref, ki_ref, k_ref, v_ref, kin_ref, kn_ref, vn_ref, o_ref = refs[:11]
    key_ref, jfin_ref, m_ref, acc_ref, qx_ref = refs[-5:]
    i = pl.program_id(1)
    q0 = q_pos_base + i * tq
    q_pos = q0 + lax.broadcasted_iota(I32, (1, QL), 1)
    q_lim = jnp.minimum((q_pos // CHUNK + 1) * CHUNK, l_valid)
    last_lim = jnp.minimum(((q0 + tq - 1) // CHUNK + 1) * CHUNK, l_valid)
    nk = (last_lim + tk - 1) // tk
    k_iota = lax.broadcasted_iota(I32, (tk, 1), 0)

    qi = qi_ref[...].reshape(IDX_HEADS * QL, IDX_DIM)
    wi = wi_ref[...]

    nk_main = nk if n_cache_tiles is None else n_cache_tiles

    def for_tiles(body, group):
        start = 0
        while group >= 1:
            def grouped(p, carry, group=group, start=start):
                for u in range(group):
                    body(start + group * p + u)
                return carry
            n_groups = (nk_main - start) // group
            lax.fori_loop(0, n_groups, grouped, 0)
            start = start + n_groups * group
            group //= 2

    def rows(ref, j):
        return ref[pl.ds(pl.multiple_of(j * tk, tk), tk), :]

    def score_tile(j, kt):
        s = _dot_nt(kt.astype(BF16), qi)
        acc = jnp.zeros((tk, QL), F32)
        for h in range(IDX_HEADS):
            acc = acc + wi[h:h + 1, :] * jnp.maximum(s[:, h * QL:(h + 1) * QL], 0.0)
        adm = (j * tk + k_iota) < q_lim
        sc = jnp.where(adm, acc, -jnp.inf)
        sc = jnp.where(sc == 0.0, 0.0, sc)
        bits = lax.bitcast_convert_type(sc, I32)
        key = bits ^ ((bits >> 31) & INT_MAX)
        key_ref[j] = key

    for_tiles(lambda j: score_tile(j, rows(ki_ref, j)), SCORE_GROUP)
    if n_cache_tiles is not None:
        score_tile(n_cache_tiles, kin_ref[...])

    def count_keys(pred):
        def count_tile(j, cnt):
            hit = pred(key_ref[j], j * tk).astype(I32)
            return cnt + jnp.sum(hit.reshape(tk // 8, 8, QL), axis=0)

        cnt = jnp.zeros((8, QL), I32)
        start, group = 0, COUNT_GROUP
        while group >= 1:
            def grouped(p, cnt, group=group, start=start):
                for u in range(group):
                    cnt = count_tile(start + group * p + u, cnt)
                return cnt
            n_groups = (nk - start) // group
            cnt = lax.fori_loop(0, n_groups, grouped, cnt)
            start = start + n_groups * group
            group //= 2
        return jnp.sum(cnt, axis=0, keepdims=True)

    def search_bit(it, carry):
        t_biased, c_ge = carry
        cand = t_biased | lax.shift_left(jnp.int32(1), 31 - it)
        cand_s = cand ^ INT_MIN
        c = count_keys(lambda kk, base: kk >= cand_s)
        ok = c >= n_sel
        return jnp.where(ok, cand, t_biased), jnp.where(ok, c, c_ge)

    t_biased, c_ge = lax.fori_loop(
        0, 32, search_bit, (jnp.zeros((1, QL), I32), jnp.zeros((1, QL), I32) + nk * tk))
    thr = t_biased ^ INT_MIN

    jfin_ref[...] = jnp.full((1, QL), INT_MAX, I32)

    @pl.when(jnp.max(c_ge) > n_sel)
    def _():
        c_gt = count_keys(lambda kk, base: kk > thr)
        need = n_sel - c_gt

        def idx_bit(it, jlo):
            cand = jlo | lax.shift_left(jnp.int32(1), idx_bits - 1 - it)
            c = count_keys(lambda kk, base: (kk == thr) & ((k_iota + base) < cand))
            return jnp.where(c < need, cand, jlo)

        jlo = lax.fori_loop(0, idx_bits, idx_bit, jnp.zeros((1, QL), I32))
        jfin_ref[...] = jnp.where(c_ge > n_sel, jlo, INT_MAX)

    jfin = jfin_ref[...]
    m_ref[...] = jnp.full(m_ref.shape, -jnp.inf, F32)
    acc_ref[...] = jnp.zeros(acc_ref.shape, F32)
    ones = jnp.ones((tk, HEAD_DIM), BF16)
    oh_row = lax.broadcasted_iota(I32, (GROUP * tq, QL), 0) & (tq - 1)
    onehot = (oh_row == lax.broadcasted_iota(I32, (GROUP * tq, QL), 1)).astype(BF16)
    for g in range(N_KV_HEADS):
        qg = q_ref[g * GROUP:(g + 1) * GROUP].reshape(GROUP * tq, HEAD_DIM)
        qx_ref[g] = jnp.concatenate([qg, onehot], axis=1)

    def attend_tile(j, kt, vt):
        kk = key_ref[j]
        kpos = j * tk + k_iota
        sel = (kk > thr) | ((kk == thr) & (kpos <= jfin))
        sel = sel & (kpos < q_lim)
        bias = jnp.where(sel, 0.0, MASKED).astype(BF16)
        kt = kt.astype(BF16)
        vt = vt.astype(BF16)
        for g in range(N_KV_HEADS):
            kx = jnp.concatenate([kt[:, g * HEAD_DIM:(g + 1) * HEAD_DIM], bias], axis=1)
            s = _dot_nt(qx_ref[g], kx)
            m_prev = m_ref[g]
            m_new = jnp.maximum(m_prev, jnp.max(s, axis=-1, keepdims=True))
            alpha = jnp.exp2(m_prev - m_new)
            p = jnp.exp2(s - jnp.tile(m_new, (1, tk // LANES))).astype(BF16)
            vx = jnp.concatenate([vt[:, g * HEAD_DIM:(g + 1) * HEAD_DIM], ones], axis=1)
            acc_ref[g] = jnp.tile(alpha, (1, 2)) * acc_ref[g] + jnp.dot(p, vx, preferred_element_type=F32)
            m_ref[g] = m_new

    for_tiles(lambda j: attend_tile(j, rows(k_ref, j), rows(v_ref, j)), ATTEND_GROUP)
    if n_cache_tiles is not None:
        attend_tile(n_cache_tiles, kn_ref[...], vn_ref[...])

    for g in range(N_KV_HEADS):
        acc = acc_ref[g]
        o = acc[:, :HEAD_DIM] / acc[:, HEAD_DIM:]
        for u in range(GROUP):
            h = g * GROUP + u
            z = z_ref[:, h * HEAD_DIM:(h + 1) * HEAD_DIM]
            gated = o[u * tq:(u + 1) * tq] * (z * _sigmoid(z))
            o_ref[:, h * HEAD_DIM:(h + 1) * HEAD_DIM] = gated.astype(BF16)


def _dsa_attention(q_hm, qi_hm, wi_hm, z, ki, k, v, new_keys=None, *, n_batch, n_qblk, tq, tk, per_batch_rows,
                   q_pos_base, l_valid):
    lk = ki.shape[1]
    assert tq & (tq - 1) == 0 and tq <= QL and lk % tk == 0
    n_tiles = lk // tk + (0 if new_keys is None else 1)
    n_sel = min(MAX_SELECT, l_valid // 4)
    idx_bits = max(1, (n_tiles * tk - 1).bit_length())
    if per_batch_rows:
        qmap = lambda b, i: (b, 0, i, 0)
        rmap = lambda b, i: (b, i, 0)
        imap = qmap
        wmap = lambda b, i: (b, 0, i)
    else:
        qmap = lambda b, i: (0, 0, b, 0)
        rmap = lambda b, i: (0, b, 0)
        imap = lambda b, i: (b, 0, 0, 0)
        wmap = lambda b, i: (b, 0, 0)
    kmap = lambda b, i: (b, 0, 0)
    kern = functools.partial(_dsa_kernel, tq=tq, tk=tk, q_pos_base=q_pos_base, l_valid=l_valid,
                             n_sel=n_sel, idx_bits=idx_bits,
                             n_cache_tiles=None if new_keys is None else lk // tk)
    key_specs = [
        pl.BlockSpec((None, lk, IDX_DIM), kmap),
        pl.BlockSpec((None, lk, KV_WIDTH), kmap),
        pl.BlockSpec((None, lk, KV_WIDTH), kmap),
    ]
    if new_keys is not None:
        key_specs += [
            pl.BlockSpec((None, tk, IDX_DIM), kmap),
            pl.BlockSpec((None, tk, KV_WIDTH), kmap),
            pl.BlockSpec((None, tk, KV_WIDTH), kmap),
        ]
    return pl.pallas_call(
        kern,
        out_shape=jax.ShapeDtypeStruct(z.shape, BF16),
        grid=(n_batch, n_qblk),
        in_specs=[
            pl.BlockSpec((None, N_HEADS, tq, HEAD_DIM), qmap),
            pl.BlockSpec((None, IDX_HEADS, QL, IDX_DIM), imap),
            pl.BlockSpec((None, IDX_HEADS, QL), wmap),
            pl.BlockSpec((None, tq, ATT_WIDTH), rmap),
        ] + key_specs,
        out_specs=pl.BlockSpec((None, tq, ATT_WIDTH), rmap),
        scratch_shapes=[
            pltpu.VMEM((n_tiles, tk, QL), I32),
            pltpu.VMEM((1, QL), I32),
            pltpu.VMEM((N_KV_HEADS, GROUP * tq, LANES), F32),
            pltpu.VMEM((N_KV_HEADS, GROUP * tq, 2 * HEAD_DIM), F32),
            pltpu.VMEM((N_KV_HEADS, GROUP * tq, HEAD_DIM + QL), BF16),
        ],
        compiler_params=_params("parallel", "arbitrary"),
        name="dsa_attention",
    )(q_hm, qi_hm, wi_hm, z, ki, k, v, *(new_keys or ()))


def _rec_in_kernel(x_ref, nw_ref, w_ref, p_ref, *, col_chunk):
    xn = _rms_scale(x_ref[...], nw_ref[...]).astype(BF16)
    for c in range(0, w_ref.shape[1], col_chunk):
        p_ref[:, c:c + col_chunk] = jnp.dot(xn, w_ref[:, c:c + col_chunk], preferred_element_type=F32)


def _rec_in_proj(x, nw, w, tm):
    B, L, D = x.shape
    n_out = w.shape[1]
    return pl.pallas_call(
        functools.partial(_rec_in_kernel, col_chunk=1024),
        out_shape=jax.ShapeDtypeStruct((B, L, n_out), F32),
        grid=(B, L // tm),
        in_specs=[
            pl.BlockSpec((None, tm, D), lambda b, i: (b, i, 0)),
            pl.BlockSpec((1, D), lambda b, i: (0, 0)),
            pl.BlockSpec(w.shape, lambda b, i: (0, 0)),
        ],
        out_specs=pl.BlockSpec((None, tm, n_out), lambda b, i: (b, i, 0)),
        compiler_params=_params("parallel", "parallel"),
        name="rec_in_proj",
    )(x, nw.reshape(1, D), w)


def _rec_kernel(qp_ref, fp_ref, v_ref, z_ref, lbl_ref, gn_ref, s0_ref, o_ref, sout_ref,
                st_ref, k_all, b_all, v_all, *, layer, ch, n_ch):
    t = pl.program_id(2)

    @pl.when(t == 0)
    def _():
        for hh in range(REC_HPS):
            st_ref[hh] = s0_ref[hh].T

    logits = lbl_ref[...]
    e = jnp.exp(logits - jnp.max(logits, axis=0, keepdims=True))
    p_lb = e / jnp.sum(e, axis=0, keepdims=True)
    lb_all = jnp.zeros((1, REC_HPS * REC_DK), F32)
    for r in range(1, layer + 1):
        lb_all = lb_all + p_lb[r:r + 1]
    gn_all = gn_ref[...]

    tri = (lax.broadcasted_iota(I32, (ch, ch), 0) >= lax.broadcasted_iota(I32, (ch, ch), 1)).astype(F32)
    half_row = lax.broadcasted_iota(I32, (SUB // 2, REC_DK), 0)
    ch_row = lax.broadcasted_iota(I32, (ch, REC_DK), 0)

    def chunk(c, st, slot, hh):
        r = pl.multiple_of(c * ch, ch)
        lanes = slice(hh * REC_DK, (hh + 1) * REC_DK)
        lb = lb_all[:, lanes]
        gn = gn_all[:, lanes]
        qp = qp_ref[pl.ds(r, ch), lanes]
        f = lb + (1.0 - lb) * _sigmoid(fp_ref[pl.ds(r, ch), lanes])
        q = qp * _sigmoid(qp)
        g = jnp.log(f)
        kk = 1.0 - f
        v = v_ref[pl.ds(r, ch), lanes]
        b = jnp.dot(tri, g, preferred_element_type=F32, precision=lax.Precision.HIGHEST)
        k_s, b_s, v_s = k_all.at[slot], b_all.at[slot], v_all.at[slot]
        k_s[...] = kk
        b_s[...] = b
        v_s[...] = v
        vb = v.astype(BF16)

        o_inter = _dot_nt((q * jnp.exp(b)).astype(BF16), st.astype(BF16))

        for blk in range(ch // SUB):
            r0 = blk * SUB
            q_i = q[r0:r0 + SUB]
            b_i = b[r0:r0 + SUB]
            o_i = o_inter[r0:r0 + SUB]
            half = SUB // 2
            o_lo, o_hi = o_i[:half], o_i[half:]
            for s in range(SUB):
                b_row = b_s[r0 + s:r0 + s + 1, :]
                k_row = k_s[r0 + s:r0 + s + 1, :]
                v_row = v_s[r0 + s:r0 + s + 1, :]
                if s < half:
                    a = q_i[:half] * (k_row * jnp.exp(b_i[:half] - b_row))
                    a = jnp.where(half_row >= s, a, 0.0)
                    o_lo = o_lo + jnp.sum(a, axis=-1, keepdims=True) * v_row
                a = q_i[half:] * (k_row * jnp.exp(b_i[half:] - b_row))
                if s > half:
                    a = jnp.where(half_row >= s - half, a, 0.0)
                o_hi = o_hi + jnp.sum(a, axis=-1, keepdims=True) * v_row
            o_i = jnp.concatenate([o_lo, o_hi], axis=0)
            if blk > 0:
                c_i = b[r0 - 1:r0]
                a = (q_i * jnp.exp(b_i - c_i)).astype(BF16)
                kd = jnp.where(ch_row < r0, kk * jnp.exp(jnp.minimum(c_i - b, 0.0)), 0.0).astype(BF16)
                pm = _dot_nt(a, kd)
                o_i = o_i + jnp.dot(pm.astype(BF16), vb, preferred_element_type=F32)
            o_n = o_i * lax.rsqrt(jnp.mean(o_i * o_i, axis=-1, keepdims=True) + EPS) * gn
            z = z_ref[pl.ds(r + r0, SUB), lanes]
            o_ref[pl.ds(r + r0, SUB), lanes] = (o_n * (z * _sigmoid(z))).astype(o_ref.dtype)

        b_last = b[ch - 1:ch]
        kd = (kk * jnp.exp(b_last - b)).astype(BF16)
        return st * jnp.exp(b_last) + _dot_tn(vb, kd)

    heads = range(REC_HPS)
    sts = tuple(st_ref[hh] for hh in heads)
    if n_ch % 2 == 0:
        def pair(c2, sts):
            return tuple(chunk(2 * c2 + 1, chunk(2 * c2, sts[hh], 2 * hh, hh), 2 * hh + 1, hh) for hh in heads)
        sts = lax.fori_loop(0, n_ch // 2, pair, sts)
    else:
        sts = lax.fori_loop(0, n_ch, lambda c, sts: tuple(chunk(c, sts[hh], 2 * hh, hh) for hh in heads), sts)
    for hh in heads:
        st_ref[hh] = sts[hh]

    @pl.when(t == pl.num_programs(2) - 1)
    def _():
        for hh in heads:
            sout_ref[hh] = st_ref[hh].T


def _rec_scan(p, lb_logits, gnorm, s0, *, layer, tl, ch):
    B, L, _ = p.shape
    H = REC_HEADS
    hw = REC_HPS * REC_DK
    col = lambda off: pl.BlockSpec((None, tl, hw), lambda b, h, t: (b, t, off // REC_HPS + h))
    n_rec = lb_logits.shape[0]
    st_spec = pl.BlockSpec((None, REC_HPS, REC_DK, REC_DV), lambda b, h, t: (b, h, 0, 0))
    return pl.pallas_call(
        functools.partial(_rec_kernel, layer=layer, ch=ch, n_ch=tl // ch),
        out_shape=(jax.ShapeDtypeStruct((B, L, H * REC_DV), BF16),
                   jax.ShapeDtypeStruct(s0.shape, F32)),
        grid=(B, H // REC_HPS, L // tl),
        in_specs=[
            col(0), col(H), col(2 * H), col(3 * H),
            pl.BlockSpec((n_rec, hw), lambda b, h, t: (0, h)),
            pl.BlockSpec((1, hw), lambda b, h, t: (0, h)),
            st_spec,
        ],
        out_specs=(pl.BlockSpec((None, tl, hw), lambda b, h, t: (b, t, h)), st_spec),
        scratch_shapes=[pltpu.VMEM((REC_HPS, REC_DV, REC_DK), F32)]
        + [pltpu.VMEM((2 * REC_HPS, ch, REC_DK), F32)] * 3,
        compiler_params=_params("parallel", "parallel", "arbitrary"),
        name="rec_scan",
    )(p, p, p, p, lb_logits, gnorm.reshape(1, H * REC_DV), s0)


def _out_kernel(a_ref, w_ref, h_ref, fw_ref, o_ref, *, final_norm):
    h = h_ref[...] + jnp.dot(a_ref[...].astype(BF16), w_ref[...], preferred_element_type=F32)
    if final_norm:
        h = _rms_scale(h, fw_ref[...])
    o_ref[...] = h


def _out_proj(a, w, h, fw, tm, final_norm):
    B, L, D = h.shape
    tok = lambda width: pl.BlockSpec((None, tm, width), lambda b, i: (b, i, 0))
    return pl.pallas_call(
        functools.partial(_out_kernel, final_norm=final_norm),
        out_shape=jax.ShapeDtypeStruct(h.shape, F32),
        grid=(B, L // tm),
        in_specs=[tok(a.shape[-1]), pl.BlockSpec(w.shape, lambda b, i: (0, 0)), tok(D),
                  pl.BlockSpec((1, D), lambda b, i: (0, 0))],
        out_specs=tok(D),
        compiler_params=_params("parallel", "parallel"),
        name="out_proj",
    )(a, w, h, fw.reshape(1, D))


def _rope_tables(pos):
    half = HEAD_DIM // 2
    inv = ROPE_THETA ** (-jnp.arange(half, dtype=F32) / half)
    ang = pos.astype(F32)[:, None] * inv[None, :]
    cos = jnp.cos(ang)
    sin = jnp.sin(ang)
    return jnp.concatenate([cos, cos], axis=-1), jnp.concatenate([-sin, sin], axis=-1)


def _row_tile(n, pref):
    return pref if n % pref == 0 else n


def kernel(x_prompt, x_sample, cache_k, cache_v, cache_kidx, state_rec, norm_w, final_norm_w,
           att_w_in, att_w_out, rec_w_in, rec_w_out, rec_gnorm_w, rec_lb_logits):
    B, L, D = x_prompt.shape
    DB, T, _ = x_sample.shape
    past = cache_k.shape[2]
    depth = norm_w.shape[0]
    n_mix = 2

    n_main = att_w_in.shape[-1] - IDX_HEADS
    w_att = att_w_in[:, :, :n_main].astype(BF16)
    w_wi_t = jnp.pad(jnp.swapaxes(att_w_in[:, :, n_main:], 1, 2), ((0, 0), (0, 16 - IDX_HEADS), (0, 0))).astype(BF16)
    w_att_out = att_w_out.astype(BF16)
    w_rec = rec_w_in.astype(BF16)
    w_rec_out = rec_w_out.astype(BF16)

    cos_p, sin_p = _rope_tables(jnp.arange(L, dtype=I32))
    cos_s, sin_s = _rope_tables(past + jnp.arange(T, dtype=I32))
    cos_s = jnp.tile(cos_s, (DB, 1))
    sin_s = jnp.tile(sin_s, (DB, 1))

    tk = 512
    assert past % tk == 0 and T <= tk

    h = x_prompt
    tm = _row_tile(L, 256)
    tq = _row_tile(L, 128)
    tl = _row_tile(L, 512)
    n_att = att_w_in.shape[0]
    new_stacks = lambda nb, nl: tuple(jnp.zeros((n_att, nb, nl, w), F32) for w in (KV_WIDTH, KV_WIDTH, IDX_DIM))
    st_p = new_stacks(B, L)
    sp = []
    for i in range(depth):
        j = i // n_mix
        last = i == depth - 1
        if i % n_mix == 0:
            q, qi, wi, z, *st_p, kb, vb, kib = _att_in_proj(h, norm_w[i], w_att[j], w_wi_t[j], cos_p, sin_p, tm,
                                                            layer=j, stacks=st_p)
            a = _dsa_attention(q, qi, wi, z, kib, kb, vb, n_batch=B, n_qblk=L // tq, tq=tq,
                               tk=min(tk, L), per_batch_rows=True, q_pos_base=0, l_valid=L)
            h = _out_proj(a, w_att_out[j], h, final_norm_w, tm, last)
        else:
            p = _rec_in_proj(h, norm_w[i], w_rec[j], tm)
            s0 = jnp.zeros((B, REC_HEADS, REC_DK, REC_DV), F32)
            a, s = _rec_scan(p, rec_lb_logits, rec_gnorm_w[j], s0, layer=j, tl=tl, ch=min(CHUNK, L))
            h = _out_proj(a, w_rec_out[j], h, final_norm_w, tm, last)
            sp.append(s)
    y_prompt = h

    n_rows = DB * T
    h = x_sample.reshape(1, n_rows, D)
    st_s = new_stacks(1, n_rows)
    ss_ = []
    for i in range(depth):
        j = i // n_mix
        last = i == depth - 1
        if i % n_mix == 0:
            q, qi, wi, z, *st_s, _, _, _ = _att_in_proj(h, norm_w[i], w_att[j], w_wi_t[j], cos_s, sin_s, n_rows,
                                                        layer=j, stacks=st_s)
            k, v, ki = (a[j] for a in st_s)
            qi = jnp.pad(jnp.swapaxes(qi.reshape(IDX_HEADS, DB, T, IDX_DIM), 0, 1),
                         ((0, 0), (0, 0), (0, QL - T), (0, 0)))
            wi = jnp.pad(jnp.swapaxes(wi.reshape(IDX_HEADS, DB, T), 0, 1), ((0, 0), (0, 0), (0, QL - T)))
            new_tile = lambda a: jnp.pad(a.reshape(DB, T, -1), ((0, 0), (0, tk - T), (0, 0)))
            a = _dsa_attention(q, qi, wi, z, cache_kidx[j], cache_k[j].reshape(DB, past, KV_WIDTH),
                               cache_v[j].reshape(DB, past, KV_WIDTH), (new_tile(ki), new_tile(k), new_tile(v)),
                               n_batch=DB, n_qblk=1, tq=T, tk=tk, per_batch_rows=False,
                               q_pos_base=past, l_valid=past + T)
            h = _out_proj(a, w_att_out[j], h, final_norm_w, n_rows, last)
        else:
            p = _rec_in_proj(h, norm_w[i], w_rec[j], n_rows)
            a, s = _rec_scan(p.reshape(DB, T, -1), rec_lb_logits, rec_gnorm_w[j], state_rec[j],
                             layer=j, tl=T, ch=T)
            h = _out_proj(a.reshape(1, n_rows, -1), w_rec_out[j], h, final_norm_w, n_rows, last)
            ss_.append(s)
    y_sample = h.reshape(DB, T, D)

    return (y_prompt, y_sample,
            st_p[0].reshape(n_att, B, L, N_KV_HEADS, HEAD_DIM), st_p[1].reshape(n_att, B, L, N_KV_HEADS, HEAD_DIM),
            st_p[2], jnp.stack(sp),
            st_s[0].reshape(n_att, DB, T, N_KV_HEADS, HEAD_DIM), st_s[1].reshape(n_att, DB, T, N_KV_HEADS, HEAD_DIM),
            st_s[2].reshape(n_att, DB, T, IDX_DIM), jnp.stack(ss_))
```
